```python
import math
import jax, jax.numpy as jnp
from jax import lax
import numpy as np

D_MODEL = 1024
BATCH = 8
SEQ = 2048
DEPTH = 1
DEC_BATCH = 128
DEC_SEQ = 4
PAST_LEN = 8192
PAGE_SIZE = 128

HEAD_DIM = 64
ROT_DIM = HEAD_DIM // 4
ROPE_THETA = 500000.0
NSA_HEADS = 8
NSA_KV_HEADS = 2
NSA_REP = NSA_HEADS // NSA_KV_HEADS
BLK_CMP = 32
BLK_SEL = 64
TOP_SEL = 16
WINDOW = 512
SB_HEADS = 8
Q_BLOCK = 128
N_GROUPS = 4
EXPERTS_PER_GROUP = 8
N_EXPERTS = N_GROUPS * EXPERTS_PER_GROUP
TOP_IN_GROUP = 2
D_EXPERT = 256
EPS = 1e-6
NEG = -1e30

NSA_Q_W = NSA_HEADS * HEAD_DIM
NSA_KV_W = NSA_KV_HEADS * HEAD_DIM
SB_W = SB_HEADS * HEAD_DIM
IN_SPLITS = (NSA_Q_W,
             NSA_Q_W + 6 * NSA_KV_W,
             NSA_Q_W + 6 * NSA_KV_W + 3 * NSA_HEADS,
             NSA_Q_W + 6 * NSA_KV_W + 3 * NSA_HEADS + 3 * SB_W)
IN_W = IN_SPLITS[-1] + 2 * D_MODEL
SCALE = HEAD_DIM ** -0.5

kernel_name = 'hybrid_nsa_stickbreak_hmoe_step'


def rms_norm(x, g):
    xf = x.astype(jnp.float32)
    y = xf * lax.rsqrt(jnp.mean(xf * xf, axis=-1, keepdims=True) + EPS)
    return (y * g.astype(jnp.float32)).astype(x.dtype)


def partial_rope(x, pos):
    half = ROT_DIM // 2
    inv = jnp.float32(ROPE_THETA) ** (-jnp.arange(half, dtype=jnp.float32) / half)
    ang = pos.astype(jnp.float32)[:, None] * inv[None, :]
    cos = jnp.cos(ang)[:, None, :]
    sin = jnp.sin(ang)[:, None, :]
    xr = x[..., :ROT_DIM].astype(jnp.float32)
    x1, x2 = xr[..., :half], xr[..., half:]
    rot = jnp.concatenate([x1 * cos - x2 * sin, x2 * cos + x1 * sin], axis=-1)
    return jnp.concatenate([rot.astype(x.dtype), x[..., ROT_DIM:]], axis=-1)


def masked_softmax(s, mask, axes):
    s = jnp.where(mask, s.astype(jnp.float32), NEG)
    m = jnp.max(s, axis=axes, keepdims=True)
    e = jnp.where(mask, jnp.exp(s - m), 0.0)
    return e / jnp.maximum(jnp.sum(e, axis=axes, keepdims=True), 1e-30)


def to_blocks(x):
    b, t = x.shape[:2]
    return x.reshape(b, t // Q_BLOCK, Q_BLOCK, *x.shape[2:]).swapaxes(0, 1)


def from_blocks(x):
    n, b, q = x.shape[:3]
    return x.swapaxes(0, 1).reshape(b, n * q, *x.shape[3:])


def project(xn, pos, w_in):
    b, t, _ = xn.shape
    proj = xn @ w_in
    q_a, kv_a, g_a, qkv_b, g_m = jnp.split(proj, IN_SPLITS, axis=-1)
    q_a = partial_rope(q_a.reshape(b, t, NSA_HEADS, HEAD_DIM), pos)
    q_a = q_a.reshape(b, t, NSA_KV_HEADS, NSA_REP, HEAD_DIM)
    kv_a = kv_a.reshape(b, t, 3, 2, NSA_KV_HEADS, HEAD_DIM)
    k_a = partial_rope(kv_a[:, :, :, 0].reshape(b, t, 3 * NSA_KV_HEADS, HEAD_DIM), pos)
    k_a = k_a.reshape(b, t, 3, NSA_KV_HEADS, HEAD_DIM)
    kv_a = jnp.stack([k_a, kv_a[:, :, :, 1]], axis=3)
    g_a = jax.nn.sigmoid(g_a.reshape(b, t, 3, NSA_KV_HEADS, NSA_REP))
    qkv_b = qkv_b.reshape(b, t, 3, SB_HEADS, HEAD_DIM)
    g_m = jax.nn.sigmoid(g_m.reshape(b, t, 2, D_MODEL))
    return (q_a, kv_a[:, :, 0], kv_a[:, :, 1], kv_a[:, :, 2], g_a,
            qkv_b[:, :, 0], qkv_b[:, :, 1:], g_m)


def compress_blocks(rows, cmp_w, cmp_pe):
    b, t = rows.shape[:2]
    blk = rows.reshape(b, t // BLK_CMP, BLK_CMP, *rows.shape[2:])
    return jnp.einsum('bnlcgd,cl->bncgd', blk + cmp_pe[:, :, None, :], cmp_w)


def cmp_attend(q, q_pos, kv_c):
    n_c = kv_c.shape[1]
    s = jnp.einsum('bqgrd,bngd->bgrqn', q, kv_c[:, :, 0]) * SCALE
    blk_end = (jnp.arange(n_c) + 1) * BLK_CMP - 1
    mask = blk_end[None, :] <= q_pos[:, None]
    p = masked_softmax(s, mask, -1)
    o = jnp.einsum('bgrqn,bngd->bqgrd', p.astype(kv_c.dtype), kv_c[:, :, 1])
    return o, p


def select_blocks(p_cmp, q_pos, n_sel_blocks):
    ratio = BLK_SEL // BLK_CMP
    imp = jnp.sum(p_cmp, axis=2)
    n_c = imp.shape[-1]
    imp = jnp.pad(imp, ((0, 0), (0, 0), (0, 0), (0, n_sel_blocks * ratio - n_c)))
    imp = imp.reshape(*imp.shape[:-1], n_sel_blocks, ratio).sum(-1)
    j = jnp.arange(n_sel_blocks)[None, :]
    cur = (q_pos // BLK_SEL)[:, None]
    forced = (j == 0) | (j == cur) | (j == cur - 1)
    valid = j * BLK_SEL <= q_pos[:, None]
    score = jnp.where(forced, imp + (NSA_REP + 1.0), imp)
    score = jnp.where(valid, score, -1.0)
    _, idx = lax.top_k(score, min(TOP_SEL, n_sel_blocks))
    return idx.transpose(0, 2, 1, 3)


def sel_attend(q, q_pos, kv_g, idx):
    s = jnp.einsum('bqgrd,bqgkld->bqgrkl', q, kv_g[..., 0, :]) * SCALE
    kpos = idx[..., None] * BLK_SEL + jnp.arange(BLK_SEL)
    mask = (kpos <= q_pos[None, :, None, None, None])[:, :, :, None]
    p = masked_softmax(s, mask, (-2, -1))
    return jnp.einsum('bqgrkl,bqgkld->bqgrd', p.astype(kv_g.dtype), kv_g[..., 1, :])


def win_attend(q, q_pos, kv, k_pos):
    s = jnp.einsum('...qgrd,...kgd->...grqk', q, kv[..., 0, :, :]) * SCALE
    dpos = q_pos[..., :, None] - k_pos[..., None, :]
    mask = (dpos >= 0) & (dpos < WINDOW) & (k_pos[..., None, :] >= 0)
    p = masked_softmax(s, mask[..., None, None, :, :], -1)
    return jnp.einsum('...grqk,...kgd->...qgrd', p.astype(kv.dtype), kv[..., 1, :, :])


def stick_breaking_weights(z, mask):
    z = z.astype(jnp.float32)
    log_beta = jax.nn.log_sigmoid(z)
    log_rest = jnp.where(mask, jax.nn.log_sigmoid(-z), 0.0)
    later = lax.cumsum(log_rest, axis=z.ndim - 1, reverse=True) - log_rest
    return jnp.where(mask, jnp.exp(log_beta + later), 0.0)


def nsa_combine(g_a, o_c, o_s, o_w):
    return (g_a[:, :, 0, :, :, None] * o_c + g_a[:, :, 1, :, :, None] * o_s
            + g_a[:, :, 2, :, :, None] * o_w)


def merge_branches(o_nsa, o_sb, g_m, w_br_nsa, w_br_sb, w_out):
    b, t = o_sb.shape[:2]
    y_a = o_nsa.reshape(b, t, NSA_Q_W) @ w_br_nsa
    y_b = o_sb.reshape(b, t, SB_W) @ w_br_sb
    return (g_m[:, :, 0] * y_a + g_m[:, :, 1] * y_b) @ w_out


def gather_blocks_prompt(blocks, idx):
    b_idx = jnp.arange(blocks.shape[0])[:, None, None, None]
    g_idx = jnp.arange(NSA_KV_HEADS)[None, None, :, None]
    return blocks[b_idx, idx, :, :, g_idx]


def gather_blocks_sample(pool, layer, page_table, tail, idx):
    halves = PAGE_SIZE // BLK_SEL
    n_past = PAST_LEN // BLK_SEL
    pool_h = pool.reshape(pool.shape[0], pool.shape[1] * halves, BLK_SEL, *pool.shape[3:])
    b_idx = jnp.arange(idx.shape[0])[:, None, None, None]
    g_idx = jnp.arange(NSA_KV_HEADS)[None, None, :, None]
    jp = jnp.minimum(idx, n_past - 1)
    phys = page_table[b_idx, jp // halves] * halves + jp % halves
    past = pool_h[layer, phys, :, :, g_idx]
    jt = jnp.clip(idx - n_past, 0, tail.shape[1] - 1)
    new = tail[b_idx, jt, :, :, g_idx]
    return jnp.where((idx < n_past)[..., None, None, None], past, new)


def sb_past_logits(q, pool, layer, page_table):
    k = pool[layer, page_table, :, 0]
    k = k.reshape(k.shape[0], -1, SB_HEADS, HEAD_DIM)
    return jnp.einsum('bqhd,bkhd->bhqk', q, k) * SCALE


def sb_past_values(a, pool, layer, page_table):
    v = pool[layer, page_table, :, 1]
    v = v.reshape(v.shape[0], -1, SB_HEADS, HEAD_DIM)
    return jnp.einsum('bhqk,bkhd->bqhd', a.astype(v.dtype), v)


def mixer_prompt(xn, w_in, cmp_w, cmp_pe, w_br_nsa, w_br_sb, w_out):
    b = xn.shape[0]
    n_qb = SEQ // Q_BLOCK
    pos = jnp.arange(SEQ, dtype=jnp.int32)
    q_a, rows_cmp, rows_sel, rows_win, g_a, q_b, rows_sb, g_m = project(xn, pos, w_in)
    pos_blk = pos.reshape(n_qb, Q_BLOCK)
    o_c, p_c = cmp_attend(q_a, pos, compress_blocks(rows_cmp, cmp_w, cmp_pe))
    idx = select_blocks(p_c, pos, SEQ // BLK_SEL)
    sel_blocks = rows_sel.reshape(b, SEQ // BLK_SEL, BLK_SEL, 2, NSA_KV_HEADS, HEAD_DIM)

    def sel_step(args):
        q_blk, p_blk, idx_blk = args
        return sel_attend(q_blk, p_blk, gather_blocks_prompt(sel_blocks, idx_blk), idx_blk)

    o_s = from_blocks(lax.map(sel_step, (to_blocks(q_a), pos_blk, to_blocks(idx))))
    padded = jnp.pad(rows_win, ((0, 0), (WINDOW, 0), (0, 0), (0, 0), (0, 0)))
    band = (jnp.arange(n_qb) * Q_BLOCK)[:, None] + jnp.arange(WINDOW + Q_BLOCK)[None, :]
    o_w = win_attend(q_a.reshape(b, n_qb, Q_BLOCK, NSA_KV_HEADS, NSA_REP, HEAD_DIM),
                     pos_blk, padded[:, band], band - WINDOW)
    o_w = o_w.reshape(b, SEQ, NSA_KV_HEADS, NSA_REP, HEAD_DIM)
    o_nsa = nsa_combine(g_a, o_c, o_s, o_w)
    k_b, v_b = rows_sb[:, :, 0], rows_sb[:, :, 1]
    k_pos = jnp.arange(SEQ)

    def sb_step(args):
        q_blk, p_blk = args
        z = jnp.einsum('bqhd,bkhd->bhqk', q_blk, k_b) * SCALE
        a = stick_breaking_weights(z, k_pos[None, :] < p_blk[:, None])
        return jnp.einsum('bhqk,bkhd->bqhd', a.astype(v_b.dtype), v_b)

    o_b = from_blocks(lax.map(sb_step, (to_blocks(q_b), pos_blk)))
    y = merge_branches(o_nsa, o_b, g_m, w_br_nsa, w_br_sb, w_out)
    w_keep = min(WINDOW, SEQ)
    return y, (rows_cmp, rows_sel, rows_win[:, SEQ - w_keep:], rows_sb)


def mixer_sample(xn, cache_cmp, cache_sel, win_buf, cache_sb, layer, page_table,
                 w_in, cmp_w, cmp_pe, w_br_nsa, w_br_sb, w_out):
    db = xn.shape[0]
    pos = PAST_LEN + jnp.arange(DEC_SEQ, dtype=jnp.int32)
    q_a, rows_cmp, rows_sel, rows_win, g_a, q_b, rows_sb, g_m = project(xn, pos, w_in)
    past_cmp = cache_cmp[layer, page_table].reshape(db, PAST_LEN, 2, NSA_KV_HEADS, HEAD_DIM)
    n_new_cmp = (DEC_SEQ // BLK_CMP) * BLK_CMP
    kv_c = jnp.concatenate([compress_blocks(past_cmp, cmp_w, cmp_pe),
                            compress_blocks(rows_cmp[:, :n_new_cmp], cmp_w, cmp_pe)], axis=1)
    o_c, p_c = cmp_attend(q_a, pos, kv_c)
    n_sel = -(-(PAST_LEN + DEC_SEQ) // BLK_SEL)
    idx = select_blocks(p_c, pos, n_sel)
    n_tail = n_sel - PAST_LEN // BLK_SEL
    tail = jnp.pad(rows_sel, ((0, 0), (0, n_tail * BLK_SEL - DEC_SEQ), (0, 0), (0, 0), (0, 0)))
    tail = tail.reshape(db, n_tail, BLK_SEL, 2, NSA_KV_HEADS, HEAD_DIM)
    o_s = sel_attend(q_a, pos, gather_blocks_sample(cache_sel, layer, page_table, tail, idx), idx)
    w_buf = win_buf.shape[1]
    kv_w = jnp.concatenate([win_buf, rows_win], axis=1)
    k_pos_w = PAST_LEN - w_buf + jnp.arange(w_buf + DEC_SEQ)
    o_w = win_attend(q_a, pos, kv_w, k_pos_w)
    o_nsa = nsa_combine(g_a, o_c, o_s, o_w)
    z = jnp.concatenate([sb_past_logits(q_b, cache_sb, layer, page_table),
                         jnp.einsum('bqhd,bkhd->bhqk', q_b, rows_sb[:, :, 0]) * SCALE], axis=-1)
    k_pos = jnp.arange(PAST_LEN + DEC_SEQ)
    a = stick_breaking_weights(z, k_pos[None, :] < pos[:, None])
    o_b = (sb_past_values(a[..., :PAST_LEN], cache_sb, layer, page_table)
           + jnp.einsum('bhqk,bkhd->bqhd', a[..., PAST_LEN:].astype(rows_sb.dtype), rows_sb[:, :, 1]))
    y = merge_branches(o_nsa, o_b, g_m, w_br_nsa, w_br_sb, w_out)
    return y, (rows_cmp, rows_sel, kv_w[:, DEC_SEQ:], rows_sb)


def hier_moe(x, w_rg, b_rg, w_re, b_re, w_gate, w_up, w_down):
    b, t, d = x.shape
    tok = x.reshape(b * t, d)
    g_logit = (tok @ w_rg + b_rg).astype(jnp.float32)
    g_prob = jax.nn.softmax(g_logit, axis=-1)
    grp = jnp.argmax(g_logit, axis=-1)
    g_w = jnp.take_along_axis(g_prob, grp[:, None], axis=-1)
    e_logit = (tok @ w_re + b_re).astype(jnp.float32).reshape(-1, N_GROUPS, EXPERTS_PER_GROUP)
    e_in = jnp.take_along_axis(e_logit, grp[:, None, None], axis=1)[:, 0]
    top_v, top_i = lax.top_k(e_in, TOP_IN_GROUP)
    w_top = jax.nn.softmax(top_v, axis=-1) * g_w
    expert = grp[:, None] * EXPERTS_PER_GROUP + top_i
    comb = jnp.sum(jax.nn.one_hot(expert, N_EXPERTS, dtype=jnp.float32) * w_top[..., None], axis=1)
    h = jax.nn.silu(jnp.einsum('nd,edf->nef', tok, w_gate)) * jnp.einsum('nd,edf->nef', tok, w_up)
    y = jnp.einsum('nef,efd->nd', h * comb[..., None].astype(h.dtype), w_down)
    return y.reshape(b, t, d)


def setup_inputs(seed: int = 0) -> dict:
    key = jax.random.key(seed)
    ks = jax.random.split(key, 24)
    n_pages = PAST_LEN // PAGE_SIZE
    n_pool = (DEC_BATCH * n_pages * 5) // 4
    w_buf = min(WINDOW, PAST_LEN)

    def nrm(k, shape, scale=1.0):
        return scale * jax.random.normal(k, shape, jnp.float32)

    kv_nsa = (DEPTH, n_pool, PAGE_SIZE, 2, NSA_KV_HEADS, HEAD_DIM)
    page_table = jax.random.permutation(ks[6], n_pool)[: DEC_BATCH * n_pages]
    return {
        'x_prompt': nrm(ks[0], (BATCH, SEQ, D_MODEL)),
        'x_sample': nrm(ks[1], (DEC_BATCH, DEC_SEQ, D_MODEL)),
        'cache_nsa_cmp': nrm(ks[2], kv_nsa),
        'cache_nsa_sel': nrm(ks[3], kv_nsa),
        'cache_nsa_win': nrm(ks[4], (DEPTH, DEC_BATCH, w_buf, 2, NSA_KV_HEADS, HEAD_DIM)),
        'cache_sb': nrm(ks[5], (DEPTH, n_pool, PAGE_SIZE, 2, SB_HEADS, HEAD_DIM)),
        'page_table': page_table.reshape(DEC_BATCH, n_pages).astype(jnp.int32),
        'norm_mix': 1.0 + nrm(ks[7], (DEPTH, D_MODEL), 0.02),
        'w_in': nrm(ks[8], (DEPTH, D_MODEL, IN_W), D_MODEL ** -0.5),
        'nsa_cmp_w': 1.0 / BLK_CMP + nrm(ks[9], (DEPTH, 2, BLK_CMP), 0.2 / BLK_CMP),
        'nsa_cmp_pe': nrm(ks[10], (DEPTH, BLK_CMP, 2, HEAD_DIM), 0.02),
        'w_br_nsa': nrm(ks[11], (DEPTH, NSA_Q_W, D_MODEL), NSA_Q_W ** -0.5),
        'w_br_sb': nrm(ks[12], (DEPTH, SB_W, D_MODEL), SB_W ** -0.5),
        'w_out': nrm(ks[13], (DEPTH, D_MODEL, D_MODEL), D_MODEL ** -0.5),
        'norm_ffn': 1.0 + nrm(ks[14], (DEPTH, D_MODEL), 0.02),
        'w_router_group': nrm(ks[15], (DEPTH, D_MODEL, N_GROUPS), D_MODEL ** -0.5),
        'b_router_group': nrm(ks[16], (DEPTH, N_GROUPS), 0.01),
        'w_router_expert': nrm(ks[17], (DEPTH, D_MODEL, N_EXPERTS), D_MODEL ** -0.5),
        'b_router_expert': nrm(ks[18], (DEPTH, N_EXPERTS), 0.01),
        'w_expert_gate': nrm(ks[19], (DEPTH, N_EXPERTS, D_MODEL, D_EXPERT), D_MODEL ** -0.5),
        'w_expert_up': nrm(ks[20], (DEPTH, N_EXPERTS, D_MODEL, D_EXPERT), D_MODEL ** -0.5),
        'w_expert_down': nrm(ks[21], (DEPTH, N_EXPERTS, D_EXPERT, D_MODEL), D_EXPERT ** -0.5),
        'norm_final': 1.0 + nrm(ks[22], (D_MODEL,), 0.02),
    }


def reference(x_prompt, x_sample, cache_nsa_cmp, cache_nsa_sel, cache_nsa_win, cache_sb, page_table,
              norm_mix, w_in, nsa_cmp_w, nsa_cmp_pe, w_br_nsa, w_br_sb, w_out, norm_ffn,
              w_router_group, b_router_group, w_router_expert, b_router_expert,
              w_expert_gate, w_expert_up, w_expert_down, norm_final):
    h_p, h_s = x_prompt, x_sample
    st_p, st_s = [], []
    for layer in range(DEPTH):
        y_p, s_p = mixer_prompt(rms_norm(h_p, norm_mix[layer]), w_in[layer], nsa_cmp_w[layer],
                                nsa_cmp_pe[layer], w_br_nsa[layer], w_br_sb[layer], w_out[layer])
        y_s, s_s = mixer_sample(rms_norm(h_s, norm_mix[layer]), cache_nsa_cmp, cache_nsa_sel,
                                cache_nsa_win[layer], cache_sb, layer, page_table, w_in[layer],
                                nsa_cmp_w[layer], nsa_cmp_pe[layer], w_br_nsa[layer], w_br_sb[layer],
                                w_out[layer])
        h_p = h_p + y_p
        h_s = h_s + y_s
        h_p = h_p + hier_moe(rms_norm(h_p, norm_ffn[layer]), w_router_group[layer], b_router_group[layer],
                             w_router_expert[layer], b_router_expert[layer], w_expert_gate[layer],
                             w_expert_up[layer], w_expert_down[layer])
        h_s = h_s + hier_moe(rms_norm(h_s, norm_ffn[layer]), w_router_group[layer], b_router_group[layer],
                             w_router_expert[layer], b_router_expert[layer], w_expert_gate[layer],
                             w_expert_up[layer], w_expert_down[layer])
        st_p.append(s_p)
        st_s.append(s_s)
    y_prompt = rms_norm(h_p, norm_final)
    y_sample = rms_norm(h_s, norm_final)
    new_cmp_prompt = jnp.stack([s[0] for s in st_p])
    new_sel_prompt = jnp.stack([s[1] for s in st_p])
    new_win_prompt = jnp.stack([s[2] for s in st_p])
    new_sb_prompt = jnp.stack([s[3] for s in st_p])
    new_cmp_sample = jnp.stack([s[0] for s in st_s])
    new_sel_sample = jnp.stack([s[1] for s in st_s])
    new_win_sample = jnp.stack([s[2] for s in st_s])
    new_sb_sample = jnp.stack([s[3] for s in st_s])
    return (y_prompt, y_sample, new_cmp_prompt, new_sel_prompt, new_win_prompt, new_sb_prompt,
            new_cmp_sample, new_sel_sample, new_win_sample, new_sb_sample)
```

```python
import functools

import jax
import jax.numpy as jnp
from jax import lax
from jax.experimental import pallas as pl
from jax.experimental.pallas import tpu as pltpu

F32 = jnp.float32
BF16 = jnp.bfloat16

D_MODEL = 1024
HEAD_DIM = 64
ROT_DIM = HEAD_DIM // 4
ROPE_THETA = 500000.0
NSA_HEADS = 8
NSA_KV_HEADS = 2
NSA_REP = NSA_HEADS // NSA_KV_HEADS
BLK_CMP = 32
BLK_SEL = 64
TOP_SEL = 16
WINDOW = 512
SB_HEADS = 8
N_GROUPS = 4
EXPERTS_PER_GROUP = 8
N_EXPERTS = N_GROUPS * EXPERTS_PER_GROUP
D_EXPERT = 256
PAGE_SIZE = 128
EPS = 1e-6
NEG = -1e30
SCALE = HEAD_DIM ** -0.5
UNSEL = -(2.0 ** 100)
LANES = 128
SUBLANES = 8
VMEM_LIMIT = 48 * 1024 * 1024

NSA_Q_W = NSA_HEADS * HEAD_DIM
NSA_KV_W = NSA_KV_HEADS * HEAD_DIM
SB_W = SB_HEADS * HEAD_DIM
C_KVA = NSA_Q_W
C_GA = C_KVA + 6 * NSA_KV_W
C_QKVB = C_GA + 3 * NSA_HEADS
C_GM = C_QKVB + 3 * SB_W
P_QA = 0
P_KVA = NSA_Q_W
P_GA = P_KVA + 6 * NSA_KV_W
P_QB = P_GA + LANES
P_KB = P_QB + SB_W
P_VB = P_KB + SB_W
P_END = P_VB + SB_W
HM_QA = 0
HM_QB = HM_QA + NSA_HEADS
HM_KB = HM_QB + SB_HEADS
HM_VB = HM_KB + SB_HEADS
HM_KVA = HM_VB + SB_HEADS
HM_N = HM_KVA + 12
R_W = 2 * LANES


def _cparams(*sem):
    return pltpu.CompilerParams(dimension_semantics=sem, vmem_limit_bytes=VMEM_LIMIT)


def _rms(x, g):
    return x * lax.rsqrt(jnp.mean(x * x, axis=-1, keepdims=True) + EPS) * g


def _dot(a, b):
    return jnp.dot(a, b, preferred_element_type=F32)


def _dot_nt(a, b):
    return lax.dot_general(a, b, (((1,), (1,)), ((), ())), preferred_element_type=F32)


def _dot_tn(a, b):
    return lax.dot_general(a, b, (((0,), (0,)), ((), ())), preferred_element_type=F32)


def _log_sigmoid(z):
    return jnp.minimum(z, 0.0) - jnp.log(1.0 + jnp.exp(-jnp.abs(z)))


def _split_bf16(x):
    hi = x.astype(BF16)
    return hi, (x - hi.astype(F32)).astype(BF16)


def _rope_table(pos):
    half = ROT_DIM // 2
    inv = jnp.float32(ROPE_THETA) ** (-jnp.arange(half, dtype=F32) / half)
    ang = pos.astype(F32)[:, None] * inv[None, :]
    cos, sin = jnp.cos(ang), jnp.sin(ang)
    l64 = jnp.arange(LANES) % HEAD_DIM
    f = l64 % half
    cos_t = jnp.where(l64 < ROT_DIM, cos[:, f], 1.0)
    sin_up = jnp.where(l64 < half, -sin[:, f], 0.0)
    sin_dn = jnp.where((l64 >= half) & (l64 < ROT_DIM), sin[:, f], 0.0)
    return jnp.concatenate([cos_t, sin_up, sin_dn], axis=1).astype(F32)


def _proj_kernel(x_ref, tab_ref, g_ref, w_ref, rows3_ref, rowsb_ref, ga_ref, hm_ref):
    xn = _rms(x_ref[...], g_ref[...])
    res = _dot(xn.astype(BF16), w_ref[...])
    cos = tab_ref[:, 0:LANES]
    sin_up = tab_ref[:, LANES:2 * LANES]
    sin_dn = tab_ref[:, 2 * LANES:3 * LANES]
    half = ROT_DIM // 2

    def rope(v):
        return v * cos + pltpu.roll(v, LANES - half, 1) * sin_up + pltpu.roll(v, half, 1) * sin_dn

    def put_heads(first, chunk):
        hm_ref[first] = chunk[:, :HEAD_DIM].astype(BF16)
        hm_ref[first + 1] = chunk[:, HEAD_DIM:].astype(BF16)

    for c in range(NSA_Q_W // LANES):
        put_heads(HM_QA + 2 * c, rope(res[:, P_QA + c * LANES:P_QA + (c + 1) * LANES]) * SCALE)
    for br in range(3):
        base = P_KVA + br * 2 * NSA_KV_W
        kk = rope(res[:, base:base + LANES])
        vv = res[:, base + LANES:base + 2 * LANES]
        rows3_ref[br, :, 0:LANES] = kk
        rows3_ref[br, :, LANES:2 * LANES] = vv
        put_heads(HM_KVA + br * 4, kk)
        put_heads(HM_KVA + br * 4 + 2, vv)
    ga_ref[...] = jax.nn.sigmoid(res[:, P_GA:P_GA + LANES])
    for c in range(SB_W // LANES):
        put_heads(HM_QB + 2 * c, res[:, P_QB + c * LANES:P_QB + (c + 1) * LANES] * SCALE)
        put_heads(HM_KB + 2 * c, res[:, P_KB + c * LANES:P_KB + (c + 1) * LANES])
        put_heads(HM_VB + 2 * c, res[:, P_VB + c * LANES:P_VB + (c + 1) * LANES])
    rowsb_ref[:, 0:SB_W] = res[:, P_KB:P_KB + SB_W]
    rowsb_ref[:, SB_W:2 * SB_W] = res[:, P_VB:P_VB + SB_W]


def _project(x_all, table, n_prompt, seq, g_mix, w_cat, tm):
    nt = x_all.shape[0]
    n_tiles = nt // tm
    n_prompt_tiles = n_prompt // tm
    n_tab_prompt = seq // tm

    def tab_map(i):
        return (jnp.where(i < n_prompt_tiles, i % n_tab_prompt, n_tab_prompt), 0)

    return pl.pallas_call(
        _proj_kernel,
        grid=(n_tiles,),
        in_specs=[
            pl.BlockSpec((tm, D_MODEL), lambda i: (i, 0)),
            pl.BlockSpec((tm, 3 * LANES), tab_map),
            pl.BlockSpec((1, D_MODEL), lambda i: (0, 0)),
            pl.BlockSpec((D_MODEL, P_END), lambda i: (0, 0)),
        ],
        out_specs=[
            pl.BlockSpec((3, tm, 2 * NSA_KV_W), lambda i: (0, i, 0)),
            pl.BlockSpec((tm, 2 * SB_W), lambda i: (i, 0)),
            pl.BlockSpec((tm, LANES), lambda i: (i, 0)),
            pl.BlockSpec((HM_N, tm, HEAD_DIM), lambda i: (0, i, 0)),
        ],
        out_shape=[
            jax.ShapeDtypeStruct((3, nt, 2 * NSA_KV_W), F32),
            jax.ShapeDtypeStruct((nt, 2 * SB_W), F32),
            jax.ShapeDtypeStruct((nt, LANES), F32),
            jax.ShapeDtypeStruct((HM_N, nt, HEAD_DIM), BF16),
        ],
        compiler_params=_cparams("parallel"),
        name="proj",
    )(x_all, table, g_mix, w_cat)


def _compress(rows, pe, w):
    n = rows.shape[0] // BLK_CMP
    blk = rows.reshape(n, BLK_CMP, rows.shape[1])
    return jnp.sum((blk + pe[None]) * w[None], axis=1)


def _rank_select(score, cand_of_lane, n_cand, stride, topk, extra=None):
    rank = jnp.zeros(score.shape, F32) if extra is None else extra
    for k in range(n_cand):
        ck = score[:, k * stride:k * stride + 1]
        rank = rank + jnp.where(ck > score, 1.0, jnp.where((ck == score) & (cand_of_lane > k), 1.0, 0.0))
    return rank < topk


def _cmp_prompt_kernel(q_ref, rows_ref, pe_ref, w_ref, oc_ref, selb_ref, kvc_ref, *, seq, tq):
    nc = seq // BLK_CMP
    n_sel = seq // BLK_SEL
    topk = min(TOP_SEL, n_sel)
    qi = pl.program_id(1)

    @pl.when(qi == 0)
    def _():
        kvc_ref[...] = jnp.zeros(kvc_ref.shape, F32)
        kvc_ref[0:nc, :] = _compress(rows_ref[...], pe_ref[...], w_ref[...])

    kvc = kvc_ref[...]
    t_idx = qi * tq + lax.broadcasted_iota(jnp.int32, (tq, LANES), 0)
    lane = lax.broadcasted_iota(jnp.int32, (tq, LANES), 1)
    vis = ((lane + 1) * BLK_CMP - 1 <= t_idx) & (lane < nc)
    m_blk = lane >> 1
    is_cand = ((lane & 1) == 0) & (m_blk < n_sel)
    cur = t_idx // BLK_SEL
    forced = (m_blk == 0) | (m_blk == cur) | (m_blk == cur - 1)
    valid = m_blk * BLK_SEL <= t_idx
    for g in range(NSA_KV_HEADS):
        kc = kvc[:, g * HEAD_DIM:(g + 1) * HEAD_DIM].astype(BF16)
        vc = kvc[:, NSA_KV_W + g * HEAD_DIM:NSA_KV_W + (g + 1) * HEAD_DIM].astype(BF16)
        imp = jnp.zeros((tq, LANES), F32)
        for r in range(NSA_REP):
            h = g * NSA_REP + r
            s = jnp.where(vis, _dot_nt(q_ref[h], kc), NEG)
            m = jnp.max(s, axis=-1, keepdims=True)
            e = jnp.where(vis, jnp.exp(s - m), 0.0)
            p = e / jnp.maximum(jnp.sum(e, axis=-1, keepdims=True), 1e-30)
            oc_ref[h] = _dot(p.astype(BF16), vc).astype(BF16)
            imp = imp + p
        imp2 = imp + pltpu.roll(imp, LANES - 1, 1)
        score = jnp.where(forced, imp2 + (NSA_REP + 1.0), imp2)
        score = jnp.where(valid, score, -1.0)
        score = jnp.where(is_cand, score, -2.0)
        sel = _rank_select(score, m_blk, n_sel, 2, topk)
        selb_ref[g] = jnp.where(is_cand & jnp.logical_not(sel), UNSEL, 0.0).astype(BF16)


def _cmp_prompt(hm, rows3, pe2, w2, batch, seq, tq):
    n_q = seq // tq
    nt = hm.shape[1]
    return pl.pallas_call(
        functools.partial(_cmp_prompt_kernel, seq=seq, tq=tq),
        grid=(batch, n_q),
        in_specs=[
            pl.BlockSpec((NSA_HEADS, tq, HEAD_DIM), lambda b, i: (0, b * n_q + i, 0)),
            pl.BlockSpec((None, seq, 2 * NSA_KV_W), lambda b, i: (0, b, 0)),
            pl.BlockSpec((BLK_CMP, 2 * NSA_KV_W), lambda b, i: (0, 0)),
            pl.BlockSpec((BLK_CMP, 2 * NSA_KV_W), lambda b, i: (0, 0)),
        ],
        out_specs=[
            pl.BlockSpec((NSA_HEADS, tq, HEAD_DIM), lambda b, i: (0, b * n_q + i, 0)),
            pl.BlockSpec((NSA_KV_HEADS, tq, LANES), lambda b, i: (0, b * n_q + i, 0)),
        ],
        out_shape=[
            jax.ShapeDtypeStruct((NSA_HEADS, nt, HEAD_DIM), BF16),
            jax.ShapeDtypeStruct((NSA_KV_HEADS, batch * seq, LANES), BF16),
        ],
        scratch_shapes=[pltpu.VMEM((LANES, 2 * NSA_KV_W), F32)],
        compiler_params=_cparams("parallel", "arbitrary"),
        name="cmp_prompt",
    )(hm, rows3, pe2, w2)


def _nsa_prompt_kernel(*refs, mode, tq, tk):
    if mode == "sel":
        q_ref, k_ref, v_ref, selb_ref, ind_ref, o_ref = refs
    else:
        q_ref, k_ref, v_ref, o_ref = refs
    qi = pl.program_id(1)
    q0 = qi * tq
    row = q0 + lax.broadcasted_iota(jnp.int32, (tq, tk), 0)
    col = lax.broadcasted_iota(jnp.int32, (tq, tk), 1)
    if mode == "sel":
        first, n_iter = 0, (q0 + tq) // tk
    else:
        first, n_iter = q0 // tk - WINDOW // tk, WINDOW // tk + tq // tk
    for g in range(NSA_KV_HEADS):
        def body(i, carry, g=g):
            ms, ls, accs = carry
            kt = first + i
            k0 = pl.multiple_of(jnp.maximum(kt, 0) * tk, tk)
            k_t = k_ref[g, pl.ds(k0, tk), :]
            v_t = v_ref[g, pl.ds(k0, tk), :]
            kpos = kt * tk + col
            dpos = row - kpos
            if mode == "sel":
                mask = dpos >= 0
                bias = _dot_nt(selb_ref[g], ind_ref[pl.ds(k0, tk), :])
            else:
                mask = (dpos >= 0) & (dpos < WINDOW) & (kpos >= 0)
            new_m, new_l, new_acc = [], [], []
            for r in range(NSA_REP):
                s = _dot_nt(q_ref[g * NSA_REP + r], k_t)
                if mode == "sel":
                    s = s + bias
                s = jnp.where(mask, s, NEG)
                m_new = jnp.maximum(ms[r], jnp.max(s, axis=-1, keepdims=True))
                alpha = jnp.exp(ms[r] - m_new)
                e = jnp.where(mask, jnp.exp(s - m_new), 0.0)
                new_m.append(m_new)
                new_l.append(alpha * ls[r] + jnp.sum(e, axis=-1, keepdims=True))
                new_acc.append(alpha * accs[r] + _dot(e.astype(BF16), v_t))
            return tuple(new_m), tuple(new_l), tuple(new_acc)

        init = (tuple(jnp.full((tq, 1), NEG, F32) for _ in range(NSA_REP)),
                tuple(jnp.zeros((tq, 1), F32) for _ in range(NSA_REP)),
                tuple(jnp.zeros((tq, HEAD_DIM), F32) for _ in range(NSA_REP)))
        ms, ls, accs = lax.fori_loop(0, n_iter, body, init)
        for r in range(NSA_REP):
            o_ref[g * NSA_REP + r] = (accs[r] / jnp.maximum(ls[r], 1e-30)).astype(BF16)


def _nsa_prompt(hm, branch, selb, ind, batch, seq, tq, tk):
    n_q = seq // tq
    nt = hm.shape[1]
    mode = "sel" if selb is not None else "win"
    kv_blk = (HM_KVA + branch * 4) // 2
    in_specs = [
        pl.BlockSpec((NSA_HEADS, tq, HEAD_DIM), lambda b, i: (0, b * n_q + i, 0)),
        pl.BlockSpec((NSA_KV_HEADS, seq, HEAD_DIM), lambda b, i: (kv_blk, b, 0)),
        pl.BlockSpec((NSA_KV_HEADS, seq, HEAD_DIM), lambda b, i: (kv_blk + 1, b, 0)),
    ]
    args = [hm, hm, hm]
    if mode == "sel":
        in_specs += [
            pl.BlockSpec((NSA_KV_HEADS, tq, LANES), lambda b, i: (0, b * n_q + i, 0)),
            pl.BlockSpec((seq, LANES), lambda b, i: (0, 0)),
        ]
        args += [selb, ind]
    return pl.pallas_call(
        functools.partial(_nsa_prompt_kernel, mode=mode, tq=tq, tk=tk),
        grid=(batch, n_q),
        in_specs=in_specs,
        out_specs=pl.BlockSpec((NSA_HEADS, tq, HEAD_DIM), lambda b, i: (0, b * n_q + i, 0)),
        out_shape=jax.ShapeDtypeStruct((NSA_HEADS, nt, HEAD_DIM), BF16),
        compiler_params=_cparams("parallel", "parallel"),
        name=mode + "_prompt",
    )(*args)


def _sb_prompt_kernel(q_ref, k_ref, v_ref, u_ref, o_ref, *, tq, tk):
    qi = pl.program_id(1)
    q0 = qi * tq
    n_kt = (q0 + tq) // tk
    row = q0 + lax.broadcasted_iota(jnp.int32, (tq, tk), 0)
    col = lax.broadcasted_iota(jnp.int32, (tq, tk), 1)
    u = u_ref[...]
    for h in range(SB_HEADS):
        q = q_ref[h]

        def body(i, carry, h=h, q=q):
            after, acc = carry
            k0 = pl.multiple_of((n_kt - 1 - i) * tk, tk)
            z = _dot_nt(q, k_ref[h, pl.ds(k0, tk), :])
            mask = row > k0 + col
            lb = _log_sigmoid(z)
            hi, lo = _split_bf16(jnp.where(mask, lb - z, 0.0))
            lt = _dot(hi, u) + _dot(lo, u)
            w = jnp.where(mask, jnp.exp(lb + lt[:, :tk] + after), 0.0)
            acc = acc + _dot(w.astype(BF16), v_ref[h, pl.ds(k0, tk), :])
            return after + lt[:, tk:], acc

        _, acc = lax.fori_loop(0, n_kt, body, (jnp.zeros((tq, tk), F32), jnp.zeros((tq, HEAD_DIM), F32)))
        o_ref[h] = acc.astype(BF16)


def _sb_prompt(hm, u2, batch, seq, tq, tk):
    n_q = seq // tq
    nt = hm.shape[1]
    return pl.pallas_call(
        functools.partial(_sb_prompt_kernel, tq=tq, tk=tk),
        grid=(batch, n_q),
        in_specs=[
            pl.BlockSpec((SB_HEADS, tq, HEAD_DIM), lambda b, i: (HM_QB // SB_HEADS, b * n_q + i, 0)),
            pl.BlockSpec((SB_HEADS, seq, HEAD_DIM), lambda b, i: (HM_KB // SB_HEADS, b, 0)),
            pl.BlockSpec((SB_HEADS, seq, HEAD_DIM), lambda b, i: (HM_VB // SB_HEADS, b, 0)),
            pl.BlockSpec((tk, 2 * tk), lambda b, i: (0, 0)),
        ],
        out_specs=pl.BlockSpec((SB_HEADS, tq, HEAD_DIM), lambda b, i: (0, b * n_q + i, 0)),
        out_shape=jax.ShapeDtypeStruct((SB_HEADS, nt, HEAD_DIM), BF16),
        compiler_params=_cparams("parallel", "parallel"),
        name="sb_prompt",
    )(hm, hm, hm, u2)


def _cmp_sample_kernel(pt_ref, *refs, pps, past_len, dec_seq):
    page_refs = refs[:pps]
    q_ref, pe_ref, w_ref, oc_ref, sel_ref, tail_ref, kvc_ref = refs[pps:]
    j = pl.program_id(1)
    nc = past_len // BLK_CMP
    n_past_sel = past_len // BLK_SEL
    topk = min(TOP_SEL, n_past_sel + 1)
    ncl = kvc_ref.shape[0]
    per_page = PAGE_SIZE // BLK_CMP

    @pl.when(j == 0)
    def _():
        kvc_ref[...] = jnp.zeros(kvc_ref.shape, F32)

    parts = [_compress(page_refs[i][...], pe_ref[...], w_ref[...]) for i in range(pps)]
    r0 = pl.multiple_of(j * (pps * per_page), pps * per_page)
    kvc_ref[pl.ds(r0, pps * per_page), :] = jnp.concatenate(parts, axis=0)

    @pl.when(j == pl.num_programs(1) - 1)
    def _():
        kvc = kvc_ref[...]
        lane = lax.broadcasted_iota(jnp.int32, (SUBLANES, ncl), 1)
        q_pos = past_len + lax.broadcasted_iota(jnp.int32, (SUBLANES, ncl), 0)
        m_blk = lane >> 1
        is_cand = ((lane & 1) == 0) & (m_blk < n_past_sel)
        cur = q_pos // BLK_SEL
        forced = (m_blk == 0) | (m_blk == cur) | (m_blk == cur - 1)
        lane32 = lax.broadcasted_iota(jnp.int32, (NSA_REP * SUBLANES, ncl), 1)
        pos32 = past_len + (lax.broadcasted_iota(jnp.int32, (NSA_REP * SUBLANES, ncl), 0) & (SUBLANES - 1))
        vis = ((lane32 + 1) * BLK_CMP - 1 <= pos32) & (lane32 < nc)
        tail_score = NSA_REP + 1.0
        for g in range(NSA_KV_HEADS):
            kc = kvc[:, g * HEAD_DIM:(g + 1) * HEAD_DIM].astype(BF16)
            vc = kvc[:, NSA_KV_W + g * HEAD_DIM:NSA_KV_W + (g + 1) * HEAD_DIM].astype(BF16)
            s = jnp.where(vis, _dot_nt(q_ref[g], kc), NEG)
            m = jnp.max(s, axis=-1, keepdims=True)
            e = jnp.where(vis, jnp.exp(s - m), 0.0)
            p = e / jnp.maximum(jnp.sum(e, axis=-1, keepdims=True), 1e-30)
            oc_ref[g] = _dot(p.astype(BF16), vc)
            imp = p[0:SUBLANES]
            for r in range(1, NSA_REP):
                imp = imp + p[r * SUBLANES:(r + 1) * SUBLANES]
            imp2 = imp + pltpu.roll(imp, ncl - 1, 1)
            score = jnp.where(forced, imp2 + (NSA_REP + 1.0), imp2)
            score = jnp.where(is_cand, score, -2.0)
            beaten_by_tail = jnp.where(score < tail_score, 1.0, 0.0)
            sel = _rank_select(score, m_blk, n_past_sel, 2, topk, extra=beaten_by_tail)
            sel_ref[g] = jnp.where(is_cand & sel, 1.0, 0.0)
            ahead_of_tail = jnp.sum(jnp.where(is_cand & (score >= tail_score), 1.0, 0.0), axis=-1, keepdims=True)
            tail_ref[g] = jnp.broadcast_to(jnp.where(ahead_of_tail < topk, 1.0, 0.0), (SUBLANES, LANES))


def _cmp_sample(page_table, cache2, q_s, pe2, w2, past_len, dec_seq, pps):
    db, n_pages = page_table.shape
    n_steps = n_pages // pps
    ncl = -(-(past_len // BLK_CMP) // LANES) * LANES

    def page_map(i):
        return lambda b, j, pt: (pt[b * n_pages + j * pps + i], 0, 0)

    grid_spec = pltpu.PrefetchScalarGridSpec(
        num_scalar_prefetch=1,
        grid=(db, n_steps),
        in_specs=[pl.BlockSpec((None, PAGE_SIZE, 2 * NSA_KV_W), page_map(i)) for i in range(pps)] + [
            pl.BlockSpec((None, NSA_KV_HEADS, NSA_REP * SUBLANES, HEAD_DIM), lambda b, j, pt: (b, 0, 0, 0)),
            pl.BlockSpec((BLK_CMP, 2 * NSA_KV_W), lambda b, j, pt: (0, 0)),
            pl.BlockSpec((BLK_CMP, 2 * NSA_KV_W), lambda b, j, pt: (0, 0)),
        ],
        out_specs=[
            pl.BlockSpec((None, NSA_KV_HEADS, NSA_REP * SUBLANES, HEAD_DIM), lambda b, j, pt: (b, 0, 0, 0)),
            pl.BlockSpec((None, NSA_KV_HEADS, SUBLANES, ncl), lambda b, j, pt: (b, 0, 0, 0)),
            pl.BlockSpec((None, NSA_KV_HEADS, SUBLANES, LANES), lambda b, j, pt: (b, 0, 0, 0)),
        ],
        scratch_shapes=[pltpu.VMEM((ncl, 2 * NSA_KV_W), F32)],
    )
    return pl.pallas_call(
        functools.partial(_cmp_sample_kernel, pps=pps, past_len=past_len, dec_seq=dec_seq),
        grid_spec=grid_spec,
        out_shape=[
            jax.ShapeDtypeStruct((db, NSA_KV_HEADS, NSA_REP * SUBLANES, HEAD_DIM), F32),
            jax.ShapeDtypeStruct((db, NSA_KV_HEADS, SUBLANES, ncl), F32),
            jax.ShapeDtypeStruct((db, NSA_KV_HEADS, SUBLANES, LANES), F32),
        ],
        compiler_params=_cparams("parallel", "arbitrary"),
        name="cmp_sample",
    )(page_table.reshape(-1), *([cache2] * pps), q_s, pe2, w2)


def _softmax_t(z_parts, vaug_parts):
    m = z_parts[0].max(axis=0, keepdims=True)
    for z in z_parts[1:]:
        m = jnp.maximum(m, z.max(axis=0, keepdims=True))
    acc = None
    for z, va in zip(z_parts, vaug_parts):
        e = jnp.where(z > 0.5 * NEG, jnp.exp(z - m), 0.0)
        part = _dot_tn(e.astype(BF16), va)
        acc = part if acc is None else acc + part
    return acc[:, :LANES] / jnp.maximum(acc[:, LANES:], 1e-30)


def _with_ones(v):
    return jnp.concatenate([v, jnp.ones(v.shape, v.dtype)], axis=1)


def _nsa_sample_kernel(pt_ref, *refs, pps, past_len):
    page_refs = refs[:pps]
    (qbd_ref, mask_ref, tail_ref, tmask_ref, win_ref, wtail_ref, wmask_ref, wtmask_ref,
     os_ref, ow_ref, z_ref, v_ref) = refs[pps:]
    j = pl.program_id(1)
    qbd = qbd_ref[...]
    half = PAGE_SIZE // 2
    for i in range(pps):
        page = page_refs[i][...]
        z = _dot(page[:, :LANES].astype(BF16), qbd)
        sel = jnp.concatenate([jnp.broadcast_to(mask_ref[2 * i:2 * i + 1, :], (half, LANES)),
                               jnp.broadcast_to(mask_ref[2 * i + 1:2 * i + 2, :], (half, LANES))], axis=0)
        r0 = pl.multiple_of((j * pps + i) * PAGE_SIZE, PAGE_SIZE)
        z_ref[pl.ds(r0, PAGE_SIZE), :] = jnp.where(sel > 0.5, z, NEG)
        v_ref[pl.ds(r0, PAGE_SIZE), :] = _with_ones(page[:, LANES:].astype(BF16))

    @pl.when(j == pl.num_programs(1) - 1)
    def _():
        tail = tail_ref[...]
        zt = jnp.where(tmask_ref[...] > 0.5, _dot(tail[:, :LANES].astype(BF16), qbd), NEG)
        os_ref[...] = _softmax_t([z_ref[...], zt], [v_ref[...], _with_ones(tail[:, LANES:].astype(BF16))])
        win = win_ref[...]
        wtail = wtail_ref[...]
        zw = jnp.where(wmask_ref[...] > 0.5, _dot(win[:, :LANES].astype(BF16), qbd), NEG)
        zwt = jnp.where(wtmask_ref[...] > 0.5, _dot(wtail[:, :LANES].astype(BF16), qbd), NEG)
        ow_ref[...] = _softmax_t([zw, zwt], [_with_ones(win[:, LANES:].astype(BF16)),
                                             _with_ones(wtail[:, LANES:].astype(BF16))])


def _nsa_sample(page_table, cache2, qbd, mask_t, tail, tmask, win2, wtail, wmask, wtmask, past_len, pps):
    db, n_pages = page_table.shape
    n_steps = n_pages // pps
    w_buf = win2.shape[1]

    def page_map(i):
        return lambda b, j, pt: (pt[b * n_pages + j * pps + i], 0, 0)

    per_b = lambda b, j, pt: (b, 0, 0)
    grid_spec = pltpu.PrefetchScalarGridSpec(
        num_scalar_prefetch=1,
        grid=(db, n_steps),
        in_specs=[pl.BlockSpec((None, PAGE_SIZE, 2 * NSA_KV_W), page_map(i)) for i in range(pps)] + [
            pl.BlockSpec((None, LANES, LANES), per_b),
            pl.BlockSpec((None, 2 * pps, LANES), lambda b, j, pt: (b, j, 0)),
            pl.BlockSpec((None, SUBLANES, 2 * NSA_KV_W), per_b),
            pl.BlockSpec((None, SUBLANES, LANES), per_b),
            pl.BlockSpec((None, w_buf, 2 * NSA_KV_W), per_b),
            pl.BlockSpec((None, SUBLANES, 2 * NSA_KV_W), per_b),
            pl.BlockSpec((w_buf, LANES), lambda b, j, pt: (0, 0)),
            pl.BlockSpec((SUBLANES, LANES), lambda b, j, pt: (0, 0)),
        ],
        out_specs=[pl.BlockSpec((None, LANES, LANES), per_b), pl.BlockSpec((None, LANES, LANES), per_b)],
        scratch_shapes=[pltpu.VMEM((past_len, LANES), F32), pltpu.VMEM((past_len, 2 * LANES), BF16)],
    )
    return pl.pallas_call(
        functools.partial(_nsa_sample_kernel, pps=pps, past_len=past_len),
        grid_spec=grid_spec,
        out_shape=[jax.ShapeDtypeStruct((db, LANES, LANES), F32), jax.ShapeDtypeStruct((db, LANES, LANES), F32)],
        compiler_params=_cparams("parallel", "arbitrary"),
        name="nsa_sample",
    )(page_table.reshape(-1), *([cache2] * pps), qbd, mask_t, tail, tmask, win2, wtail, wmask, wtmask)


def _sb_sample_kernel(pt_ref, *refs, pps):
    page_refs = refs[:pps]
    qbd_ref, tail_ref, tmask_ref, u_ref, o_ref, after_ref, acc_ref = refs[pps:]
    j = pl.program_id(1)
    qbd = qbd_ref[...]
    u = u_ref[...]

    def sweep(page, mask):
        z = _dot(page[:, :SB_W].astype(BF16), qbd)
        lb = _log_sigmoid(z)
        lr = lb - z
        if mask is not None:
            lr = jnp.where(mask, lr, 0.0)
        hi, lo = _split_bf16(lr)
        later = _dot(u, hi) + _dot(u, lo) + after_ref[0:1, :]
        w = jnp.exp(lb + later)
        if mask is not None:
            w = jnp.where(mask, w, 0.0)
        acc_ref[...] += _dot_tn(w.astype(BF16), page[:, SB_W:].astype(BF16))
        after_ref[0:1, :] = after_ref[0:1, :] + jnp.sum(lr, axis=0, keepdims=True)

    @pl.when(j == 0)
    def _():
        after_ref[...] = jnp.zeros(after_ref.shape, F32)
        acc_ref[...] = jnp.zeros(acc_ref.shape, F32)
        tail = jnp.concatenate([tail_ref[...], jnp.zeros((PAGE_SIZE - SUBLANES, 2 * SB_W), F32)], axis=0)
        sweep(tail, tmask_ref[...] > 0.5)

    for i in reversed(range(pps)):
        sweep(page_refs[i][...], None)

    @pl.when(j == pl.num_programs(1) - 1)
    def _():
        o_ref[...] = acc_ref[...]


def _sb_sample(page_table, cache2, qbd, tail, tmask, u1, pps):
    db, n_pages = page_table.shape
    n_steps = n_pages // pps

    def page_map(i):
        return lambda b, j, pt: (pt[b * n_pages + (n_steps - 1 - j) * pps + i], 0, 0)

    per_b = lambda b, j, pt: (b, 0, 0)
    grid_spec = pltpu.PrefetchScalarGridSpec(
        num_scalar_prefetch=1,
        grid=(db, n_steps),
        in_specs=[pl.BlockSpec((None, PAGE_SIZE, 2 * SB_W), page_map(i)) for i in range(pps)] + [
            pl.BlockSpec((None, SB_W, LANES), per_b),
            pl.BlockSpec((None, SUBLANES, 2 * SB_W), per_b),
            pl.BlockSpec((PAGE_SIZE, LANES), lambda b, j, pt: (0, 0)),
            pl.BlockSpec((PAGE_SIZE, PAGE_SIZE), lambda b, j, pt: (0, 0)),
        ],
        out_specs=pl.BlockSpec((None, LANES, SB_W), per_b),
        scratch_shapes=[pltpu.VMEM((SUBLANES, LANES), F32), pltpu.VMEM((LANES, SB_W), F32)],
    )
    return pl.pallas_call(
        functools.partial(_sb_sample_kernel, pps=pps),
        grid_spec=grid_spec,
        out_shape=jax.ShapeDtypeStruct((db, LANES, SB_W), F32),
        compiler_params=_cparams("parallel", "arbitrary"),
        name="sb_sample",
    )(page_table.reshape(-1), *([cache2] * pps), qbd, tail, tmask, u1)


def _route(logit):
    rows = logit.shape[0]
    gl, el = logit[:, :LANES], logit[:, LANES:]
    lane = lax.broadcasted_iota(jnp.int32, (rows, LANES), 1)
    lane_f = lane.astype(F32)
    is_group = lane < N_GROUPS
    glm = jnp.where(is_group, gl, -jnp.inf)
    gmax = jnp.max(glm, axis=-1, keepdims=True)
    grp = jnp.min(jnp.where(glm == gmax, lane_f, float(LANES)), axis=-1, keepdims=True)
    g_w = 1.0 / jnp.sum(jnp.where(is_group, jnp.exp(gl - gmax), 0.0), axis=-1, keepdims=True)
    in_grp = (lane >> 3).astype(F32) == grp
    in_grp = in_grp & (lane < N_EXPERTS)
    e1 = jnp.where(in_grp, el, -jnp.inf)
    v1 = jnp.max(e1, axis=-1, keepdims=True)
    i1 = jnp.min(jnp.where(e1 == v1, lane_f, float(LANES)), axis=-1, keepdims=True)
    e2 = jnp.where(lane_f == i1, -jnp.inf, e1)
    v2 = jnp.max(e2, axis=-1, keepdims=True)
    i2 = jnp.min(jnp.where(e2 == v2, lane_f, float(LANES)), axis=-1, keepdims=True)
    t = jnp.exp(v2 - v1)
    w1 = g_w / (1.0 + t)
    w2 = g_w * t / (1.0 + t)
    return jnp.where(lane_f == i1, w1, 0.0) + jnp.where(lane_f == i2, w2, 0.0)


def _merge_kernel(x_ref, oc_ref, os_ref, ow_ref, ob_ref, ga_ref, gmix_ref, wgm_ref, wbn_ref, wbs_ref,
                  wout_ref, gffn_ref, wrh_ref, wrl_ref, br_ref, h_ref, hn_ref, comb_ref):
    x = x_ref[...]
    xn = _rms(x, gmix_ref[...]).astype(BF16)
    gm = jax.nn.sigmoid(_dot(xn, wgm_ref[...]))
    ga = ga_ref[...]
    ya = jnp.zeros(x.shape, F32)
    yb = jnp.zeros(x.shape, F32)
    for h in range(NSA_HEADS):
        o_nsa = (ga[:, h:h + 1] * oc_ref[h].astype(F32)
                 + ga[:, NSA_HEADS + h:NSA_HEADS + h + 1] * os_ref[h].astype(F32)
                 + ga[:, 2 * NSA_HEADS + h:2 * NSA_HEADS + h + 1] * ow_ref[h].astype(F32))
        ya = ya + _dot(o_nsa.astype(BF16), wbn_ref[h])
    for h in range(SB_HEADS):
        yb = yb + _dot(ob_ref[h], wbs_ref[h])
    mix = gm[:, :D_MODEL] * ya + gm[:, D_MODEL:] * yb
    h1 = x + _dot(mix.astype(BF16), wout_ref[...])
    h_ref[...] = h1
    hn = _rms(h1, gffn_ref[...])
    hn_ref[...] = hn.astype(BF16)
    hi, lo = _split_bf16(hn)
    logit = _dot(hi, wrh_ref[...]) + _dot(hi, wrl_ref[...]) + _dot(lo, wrh_ref[...]) + br_ref[...]
    comb_ref[...] = _route(logit)


def _merge(x_all, o_c, o_s, o_w, o_b, g_a, g_mix, w_gm, w_bn, w_bs, w_out, g_ffn, wr_hi, wr_lo, b_r, tm):
    nt = x_all.shape[0]
    tok = lambda i: (i, 0)
    hm_tok = lambda i: (0, i, 0)
    const2 = lambda i: (0, 0)
    const3 = lambda i: (0, 0, 0)
    return pl.pallas_call(
        _merge_kernel,
        grid=(nt // tm,),
        in_specs=[
            pl.BlockSpec((tm, D_MODEL), tok),
            pl.BlockSpec((NSA_HEADS, tm, HEAD_DIM), hm_tok),
            pl.BlockSpec((NSA_HEADS, tm, HEAD_DIM), hm_tok),
            pl.BlockSpec((NSA_HEADS, tm, HEAD_DIM), hm_tok),
            pl.BlockSpec((SB_HEADS, tm, HEAD_DIM), hm_tok),
            pl.BlockSpec((tm, LANES), tok),
            pl.BlockSpec((1, D_MODEL), const2),
            pl.BlockSpec((D_MODEL, 2 * D_MODEL), const2),
            pl.BlockSpec((NSA_HEADS, HEAD_DIM, D_MODEL), const3),
            pl.BlockSpec((SB_HEADS, HEAD_DIM, D_MODEL), const3),
            pl.BlockSpec((D_MODEL, D_MODEL), const2),
            pl.BlockSpec((1, D_MODEL), const2),
            pl.BlockSpec((D_MODEL, R_W), const2),
            pl.BlockSpec((D_MODEL, R_W), const2),
            pl.BlockSpec((1, R_W), const2),
        ],
        out_specs=[
            pl.BlockSpec((tm, D_MODEL), tok),
            pl.BlockSpec((tm, D_MODEL), tok),
            pl.BlockSpec((tm, LANES), tok),
        ],
        out_shape=[
            jax.ShapeDtypeStruct((nt, D_MODEL), F32),
            jax.ShapeDtypeStruct((nt, D_MODEL), BF16),
            jax.ShapeDtypeStruct((nt, LANES), F32),
        ],
        compiler_params=_cparams("parallel"),
        name="merge",
    )(x_all, o_c, o_s, o_w, o_b, g_a, g_mix, w_gm, w_bn, w_bs, w_out, g_ffn, wr_hi, wr_lo, b_r)


def _moe_kernel(hn_ref, comb_ref, h_ref, wg_ref, wu_ref, wd_ref, gfin_ref, y_ref, acc_ref):
    e = pl.program_id(1)

    @pl.when(e == 0)
    def _():
        acc_ref[...] = jnp.zeros(acc_ref.shape, F32)

    hn = hn_ref[...]
    comb = comb_ref[...]
    lane = lax.broadcasted_iota(jnp.int32, comb.shape, 1)
    c = jnp.sum(jnp.where(lane == e, comb, 0.0), axis=-1, keepdims=True)
    a = _dot(hn, wg_ref[...])
    b = _dot(hn, wu_ref[...])
    hh = (a * jax.nn.sigmoid(a)) * b * c
    acc_ref[...] += _dot(hh.astype(BF16), wd_ref[...])

    @pl.when(e == pl.num_programs(1) - 1)
    def _():
        y_ref[...] = _rms(h_ref[...] + acc_ref[...], gfin_ref[...])


def _moe(hn, comb, h1, w_gate, w_up, w_down, g_fin, tm):
    nt = hn.shape[0]
    tok = lambda i, e: (i, 0)
    return pl.pallas_call(
        _moe_kernel,
        grid=(nt // tm, N_EXPERTS),
        in_specs=[
            pl.BlockSpec((tm, D_MODEL), tok),
            pl.BlockSpec((tm, LANES), tok),
            pl.BlockSpec((tm, D_MODEL), tok),
            pl.BlockSpec((None, D_MODEL, D_EXPERT), lambda i, e: (e, 0, 0)),
            pl.BlockSpec((None, D_MODEL, D_EXPERT), lambda i, e: (e, 0, 0)),
            pl.BlockSpec((None, D_EXPERT, D_MODEL), lambda i, e: (e, 0, 0)),
            pl.BlockSpec((1, D_MODEL), lambda i, e: (0, 0)),
        ],
        out_specs=pl.BlockSpec((tm, D_MODEL), tok),
        out_shape=jax.ShapeDtypeStruct((nt, D_MODEL), F32),
        scratch_shapes=[pltpu.VMEM((tm, D_MODEL), F32)],
        compiler_params=_cparams("parallel", "arbitrary"),
        name="moe",
    )(hn, comb, h1, w_gate, w_up, w_down, g_fin)


def _pick(n, prefs):
    for p in prefs:
        if n % p == 0:
            return p
    raise ValueError(f"no tile size for {n}")


def kernel(x_prompt, x_sample, cache_nsa_cmp, cache_nsa_sel, cache_nsa_win, cache_sb, page_table, norm_mix, w_in,
           nsa_cmp_w, nsa_cmp_pe, w_br_nsa, w_br_sb, w_out, norm_ffn, w_router_group, b_router_group,
           w_router_expert, b_router_expert, w_expert_gate, w_expert_up, w_expert_down, norm_final):
    batch, seq, d = x_prompt.shape
    db, ds, _ = x_sample.shape
    depth = norm_mix.shape[0]
    n_pages = page_table.shape[1]
    past_len = n_pages * PAGE_SIZE
    w_buf = cache_nsa_win.shape[2]
    assert depth == 1 and d == D_MODEL
    assert ds <= SUBLANES and ds < BLK_CMP and past_len % BLK_SEL == 0 and w_buf == WINDOW and seq >= WINDOW
    n_p, n_s = batch * seq, db * ds
    nt = n_p + n_s

    w = w_in[0]
    w_cat = jnp.concatenate([w[:, :C_GA], jnp.pad(w[:, C_GA:C_QKVB], ((0, 0), (0, LANES - 3 * NSA_HEADS))),
                             w[:, C_QKVB:C_GM]], axis=1).astype(BF16)
    w_gm = w[:, C_GM:].astype(BF16)
    w_bn = w_br_nsa[0].reshape(NSA_HEADS, HEAD_DIM, D_MODEL).astype(BF16)
    w_bs = w_br_sb[0].reshape(SB_HEADS, HEAD_DIM, D_MODEL).astype(BF16)
    w_o = w_out[0].astype(BF16)
    w_r = jnp.zeros((D_MODEL, R_W), F32).at[:, :N_GROUPS].set(w_router_group[0])
    w_r = w_r.at[:, LANES:LANES + N_EXPERTS].set(w_router_expert[0])
    wr_hi = w_r.astype(BF16)
    wr_lo = (w_r - wr_hi.astype(F32)).astype(BF16)
    b_r = jnp.zeros((1, R_W), F32).at[0, :N_GROUPS].set(b_router_group[0])
    b_r = b_r.at[0, LANES:LANES + N_EXPERTS].set(b_router_expert[0])
    g_mix = norm_mix[0].reshape(1, D_MODEL)
    g_ffn = norm_ffn[0].reshape(1, D_MODEL)
    g_fin = norm_final.reshape(1, D_MODEL)
    pe2 = jnp.repeat(nsa_cmp_pe[0], NSA_KV_HEADS, axis=1).reshape(BLK_CMP, 2 * NSA_KV_W)
    w2 = jnp.repeat(nsa_cmp_w[0].T, NSA_KV_W, axis=1)

    tm = _pick(n_s, (512, 256, 128))
    assert n_p % tm == 0 and seq % tm == 0 and tm % ds == 0
    pos_sample = past_len + jnp.arange(tm, dtype=jnp.int32) % ds
    table = _rope_table(jnp.concatenate([jnp.arange(seq, dtype=jnp.int32), pos_sample]))
    lane = jnp.arange(LANES)
    ind = ((lane[None, :] & 1) == 0) & ((lane[None, :] >> 1) == (jnp.arange(seq)[:, None] // BLK_SEL))
    ind = ind.astype(BF16)
    kk = jnp.arange(PAGE_SIZE)
    later = (kk[:, None] > kk[None, :])
    u2 = jnp.concatenate([later, jnp.ones((PAGE_SIZE, PAGE_SIZE), bool)], axis=1).astype(BF16)
    u1 = later.T.astype(BF16)

    x_all = jnp.concatenate([x_prompt.reshape(n_p, D_MODEL), x_sample.reshape(n_s, D_MODEL)], axis=0)
    rows3, rowsb, g_a, hm = _project(x_all, table, n_p, seq, g_mix, w_cat, tm)

    tq = _pick(seq, (256, 128))
    o_c, selb = _cmp_prompt(hm, rows3, pe2, w2, batch, seq, tq)
    o_s = _nsa_prompt(hm, 1, selb, ind, batch, seq, tq, tq)
    o_w = _nsa_prompt(hm, 2, None, None, batch, seq, tq, tq)
    o_b = _sb_prompt(hm, u2, batch, seq, tq, PAGE_SIZE)

    pps = _pick(n_pages, (16, 8))
    cmp2 = cache_nsa_cmp.reshape(-1, PAGE_SIZE, 2 * NSA_KV_W)
    sel2 = cache_nsa_sel.reshape(-1, PAGE_SIZE, 2 * NSA_KV_W)
    sb2 = cache_sb.reshape(-1, PAGE_SIZE, 2 * SB_W)
    win2 = cache_nsa_win[0].reshape(db, w_buf, 2 * NSA_KV_W)
    pad_q = SUBLANES - ds
    qa_s = hm[HM_QA:HM_QA + NSA_HEADS, n_p:].reshape(NSA_KV_HEADS, NSA_REP, db, ds, HEAD_DIM)
    q_rows = jnp.pad(qa_s.transpose(2, 0, 1, 3, 4), ((0, 0), (0, 0), (0, 0), (0, pad_q), (0, 0)))
    q_rows = q_rows.reshape(db, NSA_KV_HEADS, NSA_REP * SUBLANES, HEAD_DIM)
    oc_s, sel_s, tail_sel = _cmp_sample(page_table, cmp2, q_rows, pe2, w2, past_len, ds, pps)
    n_col = NSA_HEADS * ds
    eye_g = jnp.eye(NSA_KV_HEADS, dtype=BF16)
    qbd = qa_s.transpose(2, 0, 4, 1, 3).reshape(db, NSA_KV_HEADS, HEAD_DIM, 1, NSA_REP * ds)
    qbd = (qbd * eye_g[None, :, None, :, None]).reshape(db, NSA_KV_W, n_col)
    qbd = jnp.pad(qbd, ((0, 0), (0, 0), (0, LANES - n_col)))
    qb_s = hm[HM_QB:HM_QB + SB_HEADS, n_p:].reshape(SB_HEADS, db, ds, HEAD_DIM)
    eye_h = jnp.eye(SB_HEADS, dtype=BF16)
    qbd_sb = qb_s.transpose(1, 0, 3, 2).reshape(db, SB_HEADS, HEAD_DIM, 1, ds)
    qbd_sb = (qbd_sb * eye_h[None, :, None, :, None]).reshape(db, SB_W, SB_HEADS * ds)
    qbd_sb = jnp.pad(qbd_sb, ((0, 0), (0, 0), (0, LANES - SB_HEADS * ds)))
    n_past_sel = past_len // BLK_SEL
    col = jnp.arange(LANES)
    col_q = col % ds
    col_ok = col < n_col
    sel_p = sel_s[:, :, :ds, 0:2 * n_past_sel:2]
    mask_t = jnp.broadcast_to(sel_p.transpose(0, 3, 1, 2)[:, :, :, None, :],
                              (db, n_past_sel, NSA_KV_HEADS, NSA_REP, ds)).reshape(db, n_past_sel, n_col)
    mask_t = jnp.pad(mask_t, ((0, 0), (0, 0), (0, LANES - n_col)))
    key_l = jnp.arange(SUBLANES)[:, None]
    new_ok = ((key_l <= col_q[None, :]) & (key_l < ds) & col_ok[None, :]).astype(F32)
    tail_c = jnp.broadcast_to(tail_sel[:, :, :ds, 0][:, :, None, :], (db, NSA_KV_HEADS, NSA_REP, ds))
    tail_c = jnp.pad(tail_c.reshape(db, 1, n_col), ((0, 0), (0, 0), (0, LANES - n_col)))
    tmask = new_ok[None] * tail_c
    key_w = jnp.arange(w_buf)[:, None]
    d_pos = (past_len + col_q[None, :]) - (past_len - w_buf + key_w)
    wmask = ((d_pos >= 0) & (d_pos < WINDOW) & col_ok[None, :]).astype(F32)
    pad_rows = lambda a: jnp.pad(a.reshape(db, ds, -1), ((0, 0), (0, pad_q), (0, 0)))
    tail_sel_rows = pad_rows(rows3[1, n_p:])
    tail_win_rows = pad_rows(rows3[2, n_p:])
    os_s, ow_s = _nsa_sample(page_table, sel2, qbd, mask_t, tail_sel_rows, tmask, win2, tail_win_rows,
                             wmask, new_ok, past_len, pps)
    sb_new_ok = ((key_l < col_q[None, :]) & (key_l < ds) & (col[None, :] < SB_HEADS * ds)).astype(F32)
    sb_tmask = jnp.pad(sb_new_ok, ((0, PAGE_SIZE - SUBLANES), (0, 0)))
    ob_s = _sb_sample(page_table, sb2, qbd_sb, pad_rows(rowsb[n_p:]), sb_tmask, u1, _pick(n_pages, (8,)))

    def nsa_cols_to_hm(o):
        o = o[:, :n_col].reshape(db, NSA_KV_HEADS, NSA_REP, ds, NSA_KV_HEADS, HEAD_DIM)
        o = jnp.stack([o[:, g, :, :, g] for g in range(NSA_KV_HEADS)], axis=0)
        return o.transpose(0, 2, 1, 3, 4).reshape(NSA_HEADS, n_s, HEAD_DIM).astype(BF16)

    oc_hm = oc_s.reshape(db, NSA_KV_HEADS, NSA_REP, SUBLANES, HEAD_DIM)[:, :, :, :ds]
    oc_hm = oc_hm.transpose(1, 2, 0, 3, 4).reshape(NSA_HEADS, n_s, HEAD_DIM).astype(BF16)
    ob_hm = ob_s[:, :SB_HEADS * ds].reshape(db, SB_HEADS, ds, SB_HEADS, HEAD_DIM)
    ob_hm = jnp.stack([ob_hm[:, h, :, h] for h in range(SB_HEADS)], axis=0).reshape(SB_HEADS, n_s, HEAD_DIM)
    o_c = o_c.at[:, n_p:].set(oc_hm)
    o_s = o_s.at[:, n_p:].set(nsa_cols_to_hm(os_s))
    o_w = o_w.at[:, n_p:].set(nsa_cols_to_hm(ow_s))
    o_b = o_b.at[:, n_p:].set(ob_hm.astype(BF16))

    h1, hn, comb = _merge(x_all, o_c, o_s, o_w, o_b, g_a, g_mix, w_gm, w_bn, w_bs, w_o, g_ffn,
                          wr_hi, wr_lo, b_r, tm)
    y = _moe(hn, comb, h1, w_expert_gate[0].astype(BF16), w_expert_up[0].astype(BF16),
             w_expert_down[0].astype(BF16), g_fin, tm)

    kv_shape = (2, NSA_KV_HEADS, HEAD_DIM)
    rows_p = rows3[:, :n_p].reshape(3, 1, batch, seq, *kv_shape)
    rows_s = rows3[:, n_p:].reshape(3, 1, db, ds, *kv_shape)
    w_keep = min(WINDOW, seq)
    new_win_sample = jnp.concatenate([cache_nsa_win[:, :, ds:], rows_s[2]], axis=2)
    return (y[:n_p].reshape(batch, seq, D_MODEL), y[n_p:].reshape(db, ds, D_MODEL),
            rows_p[0], rows_p[1], rows_p[2][:, :, seq - w_keep:],
            rowsb[:n_p].reshape(1, batch, seq, 2, SB_HEADS, HEAD_DIM),
            rows_s[0], rows_s[1], new_win_sample,
            rowsb[n_p:].reshape(1, db, ds, 2, SB_HEADS, HEAD_DIM))
```

```python
import functools

import jax
import jax.numpy as jnp
from jax import lax
from jax.experimental import pallas as pl
from jax.experimental.pallas import tpu as pltpu

F32 = jnp.float32
BF16 = jnp.bfloat16

D_MODEL = 1024
HEAD_DIM = 64
ROT_DIM = HEAD_DIM // 4
ROT_HALF = ROT_DIM // 2
ROPE_THETA = 500000.0
NSA_HEADS = 8
NSA_KV_HEADS = 2
NSA_REP = NSA_HEADS // NSA_KV_HEADS
BLK_CMP = 32
BLK_SEL = 64
TOP_SEL = 16
WINDOW = 512
SB_HEADS = 8
N_GROUPS = 4
EXPERTS_PER_GROUP = 8
N_EXPERTS = N_GROUPS * EXPERTS_PER_GROUP
D_EXPERT = 256
PAGE_SIZE = 128
EPS = 1e-6
NEG = -1e30
SCALE = HEAD_DIM ** -0.5
UNSEL = -(2.0 ** 100)
LANES = 128
SUBLANES = 8
BF16_ROWS = 16
SB_HEADS_PER_STEP = 4
VMEM_LIMIT = 48 * 1024 * 1024

NSA_Q_W = NSA_HEADS * HEAD_DIM
NSA_KV_W = NSA_KV_HEADS * HEAD_DIM
KV_ROW = 2 * NSA_KV_W
SB_W = SB_HEADS * HEAD_DIM
N_COL = NSA_HEADS * 4
C_KVA = NSA_Q_W
C_GA = C_KVA + 3 * KV_ROW
C_QKVB = C_GA + 3 * NSA_HEADS
C_KB = C_QKVB + SB_W
C_GM = C_QKVB + 3 * SB_W
Q_QA = 0
Q_GA = NSA_Q_W
Q_QB = Q_GA + LANES
Q_END = Q_QB + SB_W
T_KB = 3 * KV_ROW
T_END = T_KB + 2 * SB_W
R_W = 2 * LANES


def _cparams(*sem):
    return pltpu.CompilerParams(dimension_semantics=sem, vmem_limit_bytes=VMEM_LIMIT)


def _rms(x, g):
    return x * lax.rsqrt(jnp.mean(x * x, axis=-1, keepdims=True) + EPS) * g


def _dot(a, b):
    return jnp.dot(a, b, preferred_element_type=F32)


def _dot_nt(a, b):
    return lax.dot_general(a, b, (((1,), (1,)), ((), ())), preferred_element_type=F32)


def _log_sigmoid(z):
    return jnp.minimum(z, 0.0) - jnp.log(1.0 + jnp.exp(-jnp.abs(z)))


def _split_bf16(x):
    hi = x.astype(BF16)
    return hi, (x - hi.astype(F32)).astype(BF16)


def _pick(n, prefs):
    for p in prefs:
        if n % p == 0:
            return p
    raise ValueError(f"no tile size for {n}")


def _rope_tables(pos):
    inv = jnp.float32(ROPE_THETA) ** (-jnp.arange(ROT_HALF, dtype=F32) / ROT_HALF)
    ang = pos.astype(F32)[:, None] * inv[None, :]
    cos, sin = jnp.cos(ang), jnp.sin(ang)
    l64 = jnp.arange(LANES) % HEAD_DIM
    f = l64 % ROT_HALF
    cos_t = jnp.where(l64 < ROT_DIM, cos[:, f], 1.0)
    sin_up = jnp.where(l64 < ROT_HALF, -sin[:, f], 0.0)
    sin_dn = jnp.where((l64 >= ROT_HALF) & (l64 < ROT_DIM), sin[:, f], 0.0)
    tok = jnp.concatenate([cos_t, sin_up, sin_dn], axis=1).astype(F32)
    feat = jnp.concatenate([cos.T, sin.T], axis=0).astype(F32)
    return tok, feat


def _proj_kernel(x_ref, tabq_ref, tabk_ref, g_ref, wq_ref, wkv_ref,
                 qhm_ref, ga_ref, kva_ref, sb_ref, kvab_ref, sbb_ref):
    xn = _rms(x_ref[...], g_ref[...]).astype(BF16)
    res = _dot(xn, wq_ref[...])
    cos = tabq_ref[:, 0:LANES]
    sin_up = tabq_ref[:, LANES:2 * LANES]
    sin_dn = tabq_ref[:, 2 * LANES:3 * LANES]

    def rope(v):
        return v * cos + pltpu.roll(v, LANES - ROT_HALF, 1) * sin_up + pltpu.roll(v, ROT_HALF, 1) * sin_dn

    def put_heads(first, chunk):
        qhm_ref[first] = chunk[:, :HEAD_DIM].astype(BF16)
        qhm_ref[first + 1] = chunk[:, HEAD_DIM:].astype(BF16)

    for c in range(NSA_Q_W // LANES):
        put_heads(2 * c, rope(res[:, Q_QA + c * LANES:Q_QA + (c + 1) * LANES]) * SCALE)
        put_heads(NSA_HEADS + 2 * c, res[:, Q_QB + c * LANES:Q_QB + (c + 1) * LANES] * SCALE)
    ga_ref[...] = jax.nn.sigmoid(res[:, Q_GA:Q_GA + LANES])

    rt = _dot_nt(wkv_ref[...], xn)
    cos_f = tabk_ref[0:ROT_HALF, :]
    sin_f = tabk_ref[ROT_HALF:ROT_DIM, :]
    for br in range(3):
        for g in range(NSA_KV_HEADS):
            base = br * KV_ROW + g * HEAD_DIM
            x1 = rt[base:base + ROT_HALF]
            x2 = rt[base + ROT_HALF:base + ROT_DIM]
            k_head = jnp.concatenate([x1 * cos_f - x2 * sin_f, x2 * cos_f + x1 * sin_f,
                                      rt[base + ROT_DIM:base + HEAD_DIM]], axis=0)
            kva_ref[br, g * HEAD_DIM:(g + 1) * HEAD_DIM, :] = k_head
            kvab_ref[base:base + HEAD_DIM, :] = k_head.astype(BF16)
        vv = rt[br * KV_ROW + NSA_KV_W:(br + 1) * KV_ROW]
        kva_ref[br, NSA_KV_W:KV_ROW, :] = vv
        kvab_ref[br * KV_ROW + NSA_KV_W:(br + 1) * KV_ROW, :] = vv.astype(BF16)
    sb = rt[T_KB:T_END]
    sb_ref[...] = sb
    sbb_ref[...] = sb.astype(BF16)


def _project(x3, tabq, tabk, g_mix, w_q, w_kv, tm):
    nb, t, _ = x3.shape
    n_t = t // tm
    tok = lambda b, i: (b * n_t + i, 0)
    const = lambda b, i: (0, 0)
    return pl.pallas_call(
        _proj_kernel,
        grid=(nb, n_t),
        in_specs=[
            pl.BlockSpec((None, tm, D_MODEL), lambda b, i: (b, i, 0)),
            pl.BlockSpec((tm, 3 * LANES), lambda b, i: (i, 0)),
            pl.BlockSpec((ROT_DIM, tm), lambda b, i: (0, i)),
            pl.BlockSpec((1, D_MODEL), const),
            pl.BlockSpec((D_MODEL, Q_END), const),
            pl.BlockSpec((T_END, D_MODEL), const),
        ],
        out_specs=[
            pl.BlockSpec((2 * NSA_HEADS, tm, HEAD_DIM), lambda b, i: (0, b * n_t + i, 0)),
            pl.BlockSpec((tm, LANES), tok),
            pl.BlockSpec((3, None, KV_ROW, tm), lambda b, i: (0, b, 0, i)),
            pl.BlockSpec((None, 2 * SB_W, tm), lambda b, i: (b, 0, i)),
            pl.BlockSpec((None, 3 * KV_ROW, tm), lambda b, i: (b, 0, i)),
            pl.BlockSpec((None, 2 * SB_W, tm), lambda b, i: (b, 0, i)),
        ],
        out_shape=[
            jax.ShapeDtypeStruct((2 * NSA_HEADS, nb * t, HEAD_DIM), BF16),
            jax.ShapeDtypeStruct((nb * t, LANES), F32),
            jax.ShapeDtypeStruct((3, nb, KV_ROW, t), F32),
            jax.ShapeDtypeStruct((nb, 2 * SB_W, t), F32),
            jax.ShapeDtypeStruct((nb, 3 * KV_ROW, t), BF16),
            jax.ShapeDtypeStruct((nb, 2 * SB_W, t), BF16),
        ],
        compiler_params=_cparams("parallel", "parallel"),
        name="proj",
    )(x3, tabq, tabk, g_mix, w_q, w_kv)


def _rank_select(score, cand_of_lane, n_cand, stride, topk, extra=None):
    rank = jnp.zeros(score.shape, F32) if extra is None else extra
    for k in range(n_cand):
        ck = score[:, k * stride:k * stride + 1]
        rank = rank + jnp.where(ck > score, 1.0, jnp.where((ck == score) & (cand_of_lane > k), 1.0, 0.0))
    return rank < topk


def _cmp_attend(q, kvc, g, vis):
    kc = kvc[g * HEAD_DIM:(g + 1) * HEAD_DIM].astype(BF16)
    vc = kvc[NSA_KV_W + g * HEAD_DIM:NSA_KV_W + (g + 1) * HEAD_DIM].astype(BF16)
    s = jnp.where(vis, _dot(q, kc), NEG)
    m = jnp.max(s, axis=-1, keepdims=True)
    e = jnp.where(vis, jnp.exp(s - m), 0.0)
    p = e / jnp.maximum(jnp.sum(e, axis=-1, keepdims=True), 1e-30)
    return p, _dot_nt(p.astype(BF16), vc)


def _cmp_prompt_kernel(q_ref, rows_ref, pe_ref, w_ref, wc_ref, oc_ref, selb_ref, kvc_ref, *, seq, tq):
    nc = seq // BLK_CMP
    n_sel = seq // BLK_SEL
    topk = min(TOP_SEL, n_sel)
    qi = pl.program_id(1)

    @pl.when(qi == 0)
    def _():
        bias = jnp.sum(pe_ref[...] * w_ref[...], axis=1, keepdims=True)
        for c in range(2):
            hi, lo = _split_bf16(rows_ref[c * NSA_KV_W:(c + 1) * NSA_KV_W, :])
            acc = _dot(hi, wc_ref[2 * c]) + _dot(lo, wc_ref[2 * c]) + _dot(hi, wc_ref[2 * c + 1])
            kvc_ref[c * NSA_KV_W:(c + 1) * NSA_KV_W, :] = acc + bias[c * NSA_KV_W:(c + 1) * NSA_KV_W]

    kvc = kvc_ref[...]
    t_idx = qi * tq + lax.broadcasted_iota(jnp.int32, (tq, LANES), 0)
    lane = lax.broadcasted_iota(jnp.int32, (tq, LANES), 1)
    vis = ((lane + 1) * BLK_CMP - 1 <= t_idx) & (lane < nc)
    m_blk = lane >> 1
    is_cand = ((lane & 1) == 0) & (m_blk < n_sel)
    cur = t_idx // BLK_SEL
    forced = (m_blk == 0) | (m_blk == cur) | (m_blk == cur - 1)
    valid = m_blk * BLK_SEL <= t_idx
    for g in range(NSA_KV_HEADS):
        imp = jnp.zeros((tq, LANES), F32)
        for r in range(NSA_REP):
            h = g * NSA_REP + r
            p, o = _cmp_attend(q_ref[h], kvc, g, vis)
            oc_ref[h] = o.astype(BF16)
            imp = imp + p
        imp2 = imp + pltpu.roll(imp, LANES - 1, 1)
        score = jnp.where(forced, imp2 + (NSA_REP + 1.0), imp2)
        score = jnp.where(valid, score, -1.0)
        score = jnp.where(is_cand, score, -2.0)
        sel = _rank_select(score, m_blk, n_sel, 2, topk)
        selb_ref[g] = jnp.where(is_cand & jnp.logical_not(sel), UNSEL, 0.0).astype(BF16)


def _cmp_prompt(qhm, kva, pe_t, w_t, wc, batch, seq, tq):
    n_q = seq // tq
    nt = qhm.shape[1]
    return pl.pallas_call(
        functools.partial(_cmp_prompt_kernel, seq=seq, tq=tq),
        grid=(batch, n_q),
        in_specs=[
            pl.BlockSpec((NSA_HEADS, tq, HEAD_DIM), lambda b, i: (0, b * n_q + i, 0)),
            pl.BlockSpec((None, None, KV_ROW, seq), lambda b, i: (0, b, 0, 0)),
            pl.BlockSpec((KV_ROW, BLK_CMP), lambda b, i: (0, 0)),
            pl.BlockSpec((KV_ROW, BLK_CMP), lambda b, i: (0, 0)),
            pl.BlockSpec((4, seq, LANES), lambda b, i: (0, 0, 0)),
        ],
        out_specs=[
            pl.BlockSpec((NSA_HEADS, tq, HEAD_DIM), lambda b, i: (0, b * n_q + i, 0)),
            pl.BlockSpec((NSA_KV_HEADS, tq, LANES), lambda b, i: (0, b * n_q + i, 0)),
        ],
        out_shape=[
            jax.ShapeDtypeStruct((NSA_HEADS, nt, HEAD_DIM), BF16),
            jax.ShapeDtypeStruct((NSA_KV_HEADS, nt, LANES), BF16),
        ],
        scratch_shapes=[pltpu.VMEM((KV_ROW, LANES), F32)],
        compiler_params=_cparams("parallel", "arbitrary"),
        name="cmp_prompt",
    )(qhm, kva, pe_t, w_t, wc)


def _nsa_prompt_kernel(*refs, mode, tq, tk):
    if mode == "sel":
        q_ref, k_ref, v_ref, selb_ref, ind_ref, o_ref = refs
    else:
        q_ref, k_ref, v_ref, o_ref = refs
    qi = pl.program_id(1)
    q0 = qi * tq
    row = q0 + lax.broadcasted_iota(jnp.int32, (tq, tk), 0)
    col = lax.broadcasted_iota(jnp.int32, (tq, tk), 1)
    if mode == "sel":
        first, n_iter = 0, (q0 + tq) // tk
    else:
        first, n_iter = q0 // tk - WINDOW // tk, WINDOW // tk + tq // tk
    for g in range(NSA_KV_HEADS):
        def body(i, carry, g=g):
            ms, ls, accs = carry
            kt = first + i
            k0 = pl.multiple_of(jnp.maximum(kt, 0) * tk, tk)
            k_t = k_ref[g * HEAD_DIM:(g + 1) * HEAD_DIM, pl.ds(k0, tk)]
            v_t = v_ref[g * HEAD_DIM:(g + 1) * HEAD_DIM, pl.ds(k0, tk)]
            kpos = kt * tk + col
            dpos = row - kpos
            if mode == "sel":
                mask = dpos >= 0
                bias = _dot(selb_ref[g], ind_ref[:, pl.ds(k0, tk)])
            else:
                mask = (dpos >= 0) & (dpos < WINDOW) & (kpos >= 0)
            new_m, new_l, new_acc = [], [], []
            for r in range(NSA_REP):
                s = _dot(q_ref[g * NSA_REP + r], k_t)
                if mode == "sel":
                    s = s + bias
                s = jnp.where(mask, s, NEG)
                m_new = jnp.maximum(ms[r], jnp.max(s, axis=-1, keepdims=True))
                alpha = jnp.exp(ms[r] - m_new)
                e = jnp.where(mask, jnp.exp(s - m_new), 0.0)
                new_m.append(m_new)
                new_l.append(alpha * ls[r] + jnp.sum(e, axis=-1, keepdims=True))
                new_acc.append(alpha * accs[r] + _dot_nt(e.astype(BF16), v_t))
            return tuple(new_m), tuple(new_l), tuple(new_acc)

        init = (tuple(jnp.full((tq, 1), NEG, F32) for _ in range(NSA_REP)),
                tuple(jnp.zeros((tq, 1), F32) for _ in range(NSA_REP)),
                tuple(jnp.zeros((tq, HEAD_DIM), F32) for _ in range(NSA_REP)))
        ms, ls, accs = lax.fori_loop(0, n_iter, body, init)
        for r in range(NSA_REP):
            o_ref[g * NSA_REP + r] = (accs[r] / jnp.maximum(ls[r], 1e-30)).astype(BF16)


def _nsa_prompt(qhm, kvab, branch, selb, ind, batch, seq, tq, tk):
    n_q = seq // tq
    nt = qhm.shape[1]
    mode = "sel" if selb is not None else "win"
    in_specs = [
        pl.BlockSpec((NSA_HEADS, tq, HEAD_DIM), lambda b, i: (0, b * n_q + i, 0)),
        pl.BlockSpec((None, NSA_KV_W, seq), lambda b, i: (b, 2 * branch, 0)),
        pl.BlockSpec((None, NSA_KV_W, seq), lambda b, i: (b, 2 * branch + 1, 0)),
    ]
    args = [qhm, kvab, kvab]
    if mode == "sel":
        in_specs += [
            pl.BlockSpec((NSA_KV_HEADS, tq, LANES), lambda b, i: (0, b * n_q + i, 0)),
            pl.BlockSpec((LANES, seq), lambda b, i: (0, 0)),
        ]
        args += [selb, ind]
    return pl.pallas_call(
        functools.partial(_nsa_prompt_kernel, mode=mode, tq=tq, tk=tk),
        grid=(batch, n_q),
        in_specs=in_specs,
        out_specs=pl.BlockSpec((NSA_HEADS, tq, HEAD_DIM), lambda b, i: (0, b * n_q + i, 0)),
        out_shape=jax.ShapeDtypeStruct((NSA_HEADS, nt, HEAD_DIM), BF16),
        compiler_params=_cparams("parallel", "parallel"),
        name=mode + "_prompt",
    )(*args)


def _sb_prompt_kernel(q_ref, k_ref, v_ref, u_ref, o_ref, *, tq, tk):
    qi = pl.program_id(1)
    q0 = qi * tq
    n_kt = (q0 + tq) // tk
    row = q0 + lax.broadcasted_iota(jnp.int32, (tq, tk), 0)
    col = lax.broadcasted_iota(jnp.int32, (tq, tk), 1)
    u = u_ref[...]
    for h0 in range(0, SB_HEADS, SB_HEADS_PER_STEP):
        heads = range(h0, h0 + SB_HEADS_PER_STEP)

        def body(i, carry, heads=heads):
            k0 = pl.multiple_of((n_kt - 1 - i) * tk, tk)
            mask = row > k0 + col
            out = []
            for (after, acc), h in zip(carry, heads):
                z = _dot(q_ref[h], k_ref[h * HEAD_DIM:(h + 1) * HEAD_DIM, pl.ds(k0, tk)])
                lb = _log_sigmoid(z)
                hi, lo = _split_bf16(jnp.where(mask, lb - z, 0.0))
                lt = _dot(hi, u) + _dot(lo, u)
                w = jnp.where(mask, jnp.exp(lb + lt[:, :tk] + after), 0.0)
                acc = acc + _dot_nt(w.astype(BF16), v_ref[h * HEAD_DIM:(h + 1) * HEAD_DIM, pl.ds(k0, tk)])
                out.append((after + lt[:, tk:], acc))
            return tuple(out)

        init = tuple((jnp.zeros((tq, tk), F32), jnp.zeros((tq, HEAD_DIM), F32)) for _ in heads)
        res = lax.fori_loop(0, n_kt, body, init)
        for (_, acc), h in zip(res, heads):
            o_ref[h] = acc.astype(BF16)


def _sb_prompt(qhm, sbb, u2, batch, seq, tq, tk):
    n_q = seq // tq
    nt = qhm.shape[1]
    return pl.pallas_call(
        functools.partial(_sb_prompt_kernel, tq=tq, tk=tk),
        grid=(batch, n_q),
        in_specs=[
            pl.BlockSpec((SB_HEADS, tq, HEAD_DIM), lambda b, i: (1, b * n_q + i, 0)),
            pl.BlockSpec((None, SB_W, seq), lambda b, i: (b, 0, 0)),
            pl.BlockSpec((None, SB_W, seq), lambda b, i: (b, 1, 0)),
            pl.BlockSpec((tk, 2 * tk), lambda b, i: (0, 0)),
        ],
        out_specs=pl.BlockSpec((SB_HEADS, tq, HEAD_DIM), lambda b, i: (0, b * n_q + i, 0)),
        out_shape=jax.ShapeDtypeStruct((SB_HEADS, nt, HEAD_DIM), BF16),
        compiler_params=_cparams("parallel", "parallel"),
        name="sb_prompt",
    )(qhm, sbb, sbb, u2)


def _cmp_sample_kernel(pt_ref, *refs, pps, past_len):
    page_refs = refs[:pps]
    q_ref, pe_ref, w_ref, wsel_ref, oc_ref, sel_ref, tail_ref, kvc_ref = refs[pps:]
    j = pl.program_id(1)
    nc = past_len // BLK_CMP
    n_past_sel = past_len // BLK_SEL
    topk = min(TOP_SEL, n_past_sel + 1)
    ncl = kvc_ref.shape[1]
    per_page = PAGE_SIZE // BLK_CMP

    pages = jnp.concatenate([page_refs[i][...].astype(BF16) for i in range(pps)], axis=0)
    part = _dot(pages, wsel_ref[...])
    lane_p = lax.broadcasted_iota(jnp.int32, (KV_ROW, LANES), 1)
    row_p = lax.broadcasted_iota(jnp.int32, (KV_ROW, LANES), 0)
    tile = jnp.zeros((KV_ROW, LANES), F32)
    for i in range(pps):
        blk = part[i * KV_ROW:(i + 1) * KV_ROW]
        blk = jnp.where(row_p < NSA_KV_W, blk, pltpu.roll(blk, LANES - per_page, 1))
        if i:
            blk = pltpu.roll(blk, per_page * i, 1)
        tile = tile + jnp.where((lane_p >= per_page * i) & (lane_p < per_page * (i + 1)), blk, 0.0)
    bias = jnp.sum(pe_ref[...] * w_ref[...], axis=1, keepdims=True)
    kvc_ref[:, pl.ds(pl.multiple_of(j * LANES, LANES), LANES)] = tile + bias

    @pl.when(j == pl.num_programs(1) - 1)
    def _():
        kvc = kvc_ref[...]
        lane = lax.broadcasted_iota(jnp.int32, (SUBLANES, ncl), 1)
        q_pos = past_len + lax.broadcasted_iota(jnp.int32, (SUBLANES, ncl), 0)
        m_blk = lane >> 1
        is_cand = ((lane & 1) == 0) & (m_blk < n_past_sel)
        cur = q_pos // BLK_SEL
        forced = (m_blk == 0) | (m_blk == cur) | (m_blk == cur - 1)
        lane32 = lax.broadcasted_iota(jnp.int32, (NSA_REP * SUBLANES, ncl), 1)
        pos32 = past_len + (lax.broadcasted_iota(jnp.int32, (NSA_REP * SUBLANES, ncl), 0) & (SUBLANES - 1))
        vis = ((lane32 + 1) * BLK_CMP - 1 <= pos32) & (lane32 < nc)
        tail_score = NSA_REP + 1.0
        for g in range(NSA_KV_HEADS):
            p, o = _cmp_attend(q_ref[g], kvc, g, vis)
            oc_ref[g] = o
            imp = p[0:SUBLANES]
            for r in range(1, NSA_REP):
                imp = imp + p[r * SUBLANES:(r + 1) * SUBLANES]
            imp2 = imp + pltpu.roll(imp, ncl - 1, 1)
            score = jnp.where(forced, imp2 + (NSA_REP + 1.0), imp2)
            score = jnp.where(is_cand, score, -2.0)
            beaten_by_tail = jnp.where(score < tail_score, 1.0, 0.0)
            sel = _rank_select(score, m_blk, n_past_sel, 2, topk, extra=beaten_by_tail)
            sel_ref[g] = jnp.where(is_cand & sel, 1.0, 0.0)
            ahead_of_tail = jnp.sum(jnp.where(is_cand & (score >= tail_score), 1.0, 0.0), axis=-1, keepdims=True)
            tail_ref[g] = jnp.broadcast_to(jnp.where(ahead_of_tail < topk, 1.0, 0.0), (SUBLANES, LANES))


def _cmp_sample(page_table, cache_t, q_s, pe_t, w_t, wsel, past_len, pps):
    db, n_pages = page_table.shape
    n_steps = n_pages // pps
    ncl = n_steps * LANES

    def page_map(i):
        return lambda b, j, pt: (pt[b * n_pages + j * pps + i], 0, 0)

    grid_spec = pltpu.PrefetchScalarGridSpec(
        num_scalar_prefetch=1,
        grid=(db, n_steps),
        in_specs=[pl.BlockSpec((None, KV_ROW, PAGE_SIZE), page_map(i)) for i in range(pps)] + [
            pl.BlockSpec((None, NSA_KV_HEADS, NSA_REP * SUBLANES, HEAD_DIM), lambda b, j, pt: (b, 0, 0, 0)),
            pl.BlockSpec((KV_ROW, BLK_CMP), lambda b, j, pt: (0, 0)),
            pl.BlockSpec((KV_ROW, BLK_CMP), lambda b, j, pt: (0, 0)),
            pl.BlockSpec((PAGE_SIZE, LANES), lambda b, j, pt: (0, 0)),
        ],
        out_specs=[
            pl.BlockSpec((None, NSA_KV_HEADS, NSA_REP * SUBLANES, HEAD_DIM), lambda b, j, pt: (b, 0, 0, 0)),
            pl.BlockSpec((None, NSA_KV_HEADS, SUBLANES, ncl), lambda b, j, pt: (b, 0, 0, 0)),
            pl.BlockSpec((None, NSA_KV_HEADS, SUBLANES, LANES), lambda b, j, pt: (b, 0, 0, 0)),
        ],
        scratch_shapes=[pltpu.VMEM((KV_ROW, ncl), F32)],
    )
    return pl.pallas_call(
        functools.partial(_cmp_sample_kernel, pps=pps, past_len=past_len),
        grid_spec=grid_spec,
        out_shape=[
            jax.ShapeDtypeStruct((db, NSA_KV_HEADS, NSA_REP * SUBLANES, HEAD_DIM), F32),
            jax.ShapeDtypeStruct((db, NSA_KV_HEADS, SUBLANES, ncl), F32),
            jax.ShapeDtypeStruct((db, NSA_KV_HEADS, SUBLANES, LANES), F32),
        ],
        compiler_params=_cparams("parallel", "arbitrary"),
        name="cmp_sample",
    )(page_table.reshape(-1), *([cache_t] * pps), q_s, pe_t, w_t, wsel)


def _softmax_cols(z_parts, vaug_parts):
    m = z_parts[0].max(axis=1, keepdims=True)
    for z in z_parts[1:]:
        m = jnp.maximum(m, z.max(axis=1, keepdims=True))
    acc = None
    for z, va in zip(z_parts, vaug_parts):
        e = jnp.where(z > 0.5 * NEG, jnp.exp(z - m), 0.0)
        part = _dot_nt(va, e.astype(BF16))
        acc = part if acc is None else acc + part
    nf = acc.shape[0] - BF16_ROWS
    return acc[:nf] / jnp.maximum(acc[nf:nf + 1], 1e-30)


def _with_ones_rows(v):
    return jnp.concatenate([v, jnp.ones((BF16_ROWS, v.shape[1]), v.dtype)], axis=0)


def _nsa_sample_kernel(pt_ref, *refs, pps, w_new):
    page_refs = refs[:pps]
    (qbd_ref, selc_ref, ind_ref, tail_ref, tmask_ref, win_ref, wtail_ref, wmask_ref, wtmask_ref, wnew_ref,
     os_ref, ow_ref, nw_ref, z_ref, v_ref) = refs[pps:]
    j = pl.program_id(1)
    qbd = qbd_ref[...]
    keys = pps * PAGE_SIZE
    k_cat = jnp.concatenate([page_refs[i][0].astype(BF16) for i in range(pps)], axis=1)
    z = _dot(qbd, k_cat)
    sel = _dot(selc_ref[...].astype(BF16), ind_ref[...])
    c0 = pl.multiple_of(j * keys, keys)
    z_ref[:, pl.ds(c0, keys)] = jnp.where(sel > 0.5, z, NEG)
    v_cat = jnp.concatenate([page_refs[i][1].astype(BF16) for i in range(pps)], axis=1)
    v_ref[:, pl.ds(c0, keys)] = _with_ones_rows(v_cat)

    @pl.when(j == pl.num_programs(1) - 1)
    def _():
        zt = jnp.where(tmask_ref[...] > 0.5, _dot(qbd, tail_ref[0].astype(BF16)), NEG)
        os_ref[...] = _softmax_cols([z_ref[...], zt], [v_ref[...], _with_ones_rows(tail_ref[1].astype(BF16))])
        wk = win_ref[0]
        wv = win_ref[1]
        zw = jnp.where(wmask_ref[...] > 0.5, _dot(qbd, wk.astype(BF16)), NEG)
        zwt = jnp.where(wtmask_ref[...] > 0.5, _dot(qbd, wtail_ref[0].astype(BF16)), NEG)
        ow_ref[...] = _softmax_cols([zw, zwt], [_with_ones_rows(wv.astype(BF16)),
                                                _with_ones_rows(wtail_ref[1].astype(BF16))])
        w_buf = wk.shape[1]
        lane = lax.broadcasted_iota(jnp.int32, (KV_ROW, LANES), 1)
        shifted = pltpu.roll(win_ref[...].reshape(KV_ROW, w_buf), w_buf - w_new, 1)
        nw_ref[:, 0:w_buf - LANES] = shifted[:, 0:w_buf - LANES]
        nw_ref[:, w_buf - LANES:w_buf] = jnp.where(lane < LANES - w_new, shifted[:, w_buf - LANES:w_buf],
                                                   wnew_ref[...])


def _nsa_sample(page_table, cache_t, qbd, sel_c, ind, tail, tmask, win_t, wtail, wmask, wtmask, wnew,
                past_len, w_new, pps):
    db, n_pages = page_table.shape
    n_steps = n_pages // pps
    w_buf = win_t.shape[3]
    n_feat = NSA_KV_W + BF16_ROWS

    def page_map(i):
        return lambda b, j, pt: (pt[b * n_pages + j * pps + i], 0, 0, 0)

    per_b3 = lambda b, j, pt: (b, 0, 0)
    per_b4 = lambda b, j, pt: (b, 0, 0, 0)
    const2 = lambda b, j, pt: (0, 0)
    grid_spec = pltpu.PrefetchScalarGridSpec(
        num_scalar_prefetch=1,
        grid=(db, n_steps),
        in_specs=[pl.BlockSpec((None, 2, NSA_KV_W, PAGE_SIZE), page_map(i)) for i in range(pps)] + [
            pl.BlockSpec((None, N_COL, NSA_KV_W), per_b3),
            pl.BlockSpec((None, None, N_COL, 2 * pps), lambda b, j, pt: (b, j, 0, 0)),
            pl.BlockSpec((2 * pps, pps * PAGE_SIZE), const2),
            pl.BlockSpec((None, 2, NSA_KV_W, LANES), per_b4),
            pl.BlockSpec((None, N_COL, LANES), per_b3),
            pl.BlockSpec((None, 2, NSA_KV_W, w_buf), per_b4),
            pl.BlockSpec((None, 2, NSA_KV_W, LANES), per_b4),
            pl.BlockSpec((N_COL, w_buf), const2),
            pl.BlockSpec((N_COL, LANES), const2),
            pl.BlockSpec((None, KV_ROW, LANES), per_b3),
        ],
        out_specs=[
            pl.BlockSpec((None, NSA_KV_W, N_COL), per_b3),
            pl.BlockSpec((None, NSA_KV_W, N_COL), per_b3),
            pl.BlockSpec((None, KV_ROW, w_buf), per_b3),
        ],
        scratch_shapes=[pltpu.VMEM((N_COL, past_len), F32), pltpu.VMEM((n_feat, past_len), BF16)],
    )
    return pl.pallas_call(
        functools.partial(_nsa_sample_kernel, pps=pps, w_new=w_new),
        grid_spec=grid_spec,
        out_shape=[
            jax.ShapeDtypeStruct((db, NSA_KV_W, N_COL), F32),
            jax.ShapeDtypeStruct((db, NSA_KV_W, N_COL), F32),
            jax.ShapeDtypeStruct((db, KV_ROW, w_buf), F32),
        ],
        compiler_params=_cparams("parallel", "arbitrary"),
        name="nsa_sample",
    )(page_table.reshape(-1), *([cache_t] * pps), qbd, sel_c, ind, tail, tmask, win_t, wtail, wmask, wtmask, wnew)


def _sb_sample_kernel(pt_ref, *refs, pps):
    page_refs = refs[:pps]
    qbd_ref, tail_ref, tmask_ref, u_ref, o_ref, after_ref, acc_ref = refs[pps:]
    j = pl.program_id(1)
    qbd = qbd_ref[...]
    u = u_ref[...]

    def sweep(k_parts, v_parts, mask):
        n = len(k_parts)
        z = _dot(qbd, jnp.concatenate(k_parts, axis=1) if n > 1 else k_parts[0])
        lb = _log_sigmoid(z)
        lr = lb - z
        if mask is not None:
            lr = jnp.where(mask, lr, 0.0)
        stacked = jnp.concatenate([lr[:, t * LANES:(t + 1) * LANES] for t in range(n)], axis=0) if n > 1 else lr
        hi, lo = _split_bf16(stacked)
        lt = _dot(hi, u) + _dot(lo, u)
        after = after_ref[...]
        later = [None] * n
        for t in reversed(range(n)):
            later[t] = lt[t * N_COL:(t + 1) * N_COL, :LANES] + after
            after = after + lt[t * N_COL:(t + 1) * N_COL, LANES:]
        after_ref[...] = after
        w = jnp.exp(lb + (jnp.concatenate(later, axis=1) if n > 1 else later[0]))
        if mask is not None:
            w = jnp.where(mask, w, 0.0)
        v_cat = jnp.concatenate(v_parts, axis=1) if n > 1 else v_parts[0]
        acc_ref[...] += _dot_nt(v_cat, w.astype(BF16))

    @pl.when(j == 0)
    def _():
        after_ref[...] = jnp.zeros(after_ref.shape, F32)
        acc_ref[...] = jnp.zeros(acc_ref.shape, F32)
        sweep([tail_ref[0].astype(BF16)], [tail_ref[1].astype(BF16)], tmask_ref[...] > 0.5)

    sweep([page_refs[i][0].astype(BF16) for i in range(pps)],
          [page_refs[i][1].astype(BF16) for i in range(pps)], None)

    @pl.when(j == pl.num_programs(1) - 1)
    def _():
        o_ref[...] = acc_ref[...]


def _sb_sample(page_table, cache_t, qbd, tail, tmask, u2, pps):
    db, n_pages = page_table.shape
    n_steps = n_pages // pps

    def page_map(i):
        return lambda b, j, pt: (pt[b * n_pages + (n_steps - 1 - j) * pps + i], 0, 0, 0)

    per_b3 = lambda b, j, pt: (b, 0, 0)
    grid_spec = pltpu.PrefetchScalarGridSpec(
        num_scalar_prefetch=1,
        grid=(db, n_steps),
        in_specs=[pl.BlockSpec((None, 2, SB_W, PAGE_SIZE), page_map(i)) for i in range(pps)] + [
            pl.BlockSpec((None, N_COL, SB_W), per_b3),
            pl.BlockSpec((None, 2, SB_W, LANES), lambda b, j, pt: (b, 0, 0, 0)),
            pl.BlockSpec((N_COL, LANES), lambda b, j, pt: (0, 0)),
            pl.BlockSpec((PAGE_SIZE, 2 * PAGE_SIZE), lambda b, j, pt: (0, 0)),
        ],
        out_specs=pl.BlockSpec((None, SB_W, N_COL), per_b3),
        scratch_shapes=[pltpu.VMEM((N_COL, LANES), F32), pltpu.VMEM((SB_W, N_COL), F32)],
    )
    return pl.pallas_call(
        functools.partial(_sb_sample_kernel, pps=pps),
        grid_spec=grid_spec,
        out_shape=jax.ShapeDtypeStruct((db, SB_W, N_COL), F32),
        compiler_params=_cparams("parallel", "arbitrary"),
        name="sb_sample",
    )(page_table.reshape(-1), *([cache_t] * pps), qbd, tail, tmask, u2)


def _route(logit):
    rows = logit.shape[0]
    gl, el = logit[:, :LANES], logit[:, LANES:]
    lane = lax.broadcasted_iota(jnp.int32, (rows, LANES), 1)
    lane_f = lane.astype(F32)
    is_group = lane < N_GROUPS
    glm = jnp.where(is_group, gl, -jnp.inf)
    gmax = jnp.max(glm, axis=-1, keepdims=True)
    grp = jnp.min(jnp.where(glm == gmax, lane_f, float(LANES)), axis=-1, keepdims=True)
    g_w = 1.0 / jnp.sum(jnp.where(is_group, jnp.exp(gl - gmax), 0.0), axis=-1, keepdims=True)
    in_grp = ((lane >> 3).astype(F32) == grp) & (lane < N_EXPERTS)
    e1 = jnp.where(in_grp, el, -jnp.inf)
    v1 = jnp.max(e1, axis=-1, keepdims=True)
    i1 = jnp.min(jnp.where(e1 == v1, lane_f, float(LANES)), axis=-1, keepdims=True)
    e2 = jnp.where(lane_f == i1, -jnp.inf, e1)
    v2 = jnp.max(e2, axis=-1, keepdims=True)
    i2 = jnp.min(jnp.where(e2 == v2, lane_f, float(LANES)), axis=-1, keepdims=True)
    t = jnp.exp(v2 - v1)
    w1 = g_w / (1.0 + t)
    w2 = g_w * t / (1.0 + t)
    return jnp.where(lane_f == i1, w1, 0.0) + jnp.where(lane_f == i2, w2, 0.0)


def _merge_kernel(x_ref, oc_ref, os_ref, ow_ref, ob_ref, ga_ref, gmix_ref, wgm_ref, wbn_ref, wbs_ref,
                  wout_ref, gffn_ref, wrh_ref, wrl_ref, br_ref, h_ref, hn_ref, comb_ref):
    x = x_ref[...]
    xn = _rms(x, gmix_ref[...]).astype(BF16)
    gm = jax.nn.sigmoid(_dot(xn, wgm_ref[...]))
    ga = ga_ref[...]
    ya = jnp.zeros(x.shape, F32)
    yb = jnp.zeros(x.shape, F32)
    for h in range(NSA_HEADS):
        o_nsa = (ga[:, h:h + 1] * oc_ref[h].astype(F32)
                 + ga[:, NSA_HEADS + h:NSA_HEADS + h + 1] * os_ref[h].astype(F32)
                 + ga[:, 2 * NSA_HEADS + h:2 * NSA_HEADS + h + 1] * ow_ref[h].astype(F32))
        ya = ya + _dot(o_nsa.astype(BF16), wbn_ref[h])
    for h in range(SB_HEADS):
        yb = yb + _dot(ob_ref[h], wbs_ref[h])
    mix = gm[:, :D_MODEL] * ya + gm[:, D_MODEL:] * yb
    h1 = x + _dot(mix.astype(BF16), wout_ref[...])
    h_ref[...] = h1
    hn = _rms(h1, gffn_ref[...])
    hn_ref[...] = hn.astype(BF16)
    hi, lo = _split_bf16(hn)
    logit = _dot(hi, wrh_ref[...]) + _dot(hi, wrl_ref[...]) + _dot(lo, wrh_ref[...]) + br_ref[...]
    comb_ref[...] = _route(logit)


def _merge(x2, o_c, o_s, o_w, o_b, g_a, g_mix, w_gm, w_bn, w_bs, w_out, g_ffn, wr_hi, wr_lo, b_r, tm):
    nt = x2.shape[0]
    tok = lambda i: (i, 0)
    hm_tok = lambda i: (0, i, 0)
    const2 = lambda i: (0, 0)
    const3 = lambda i: (0, 0, 0)
    return pl.pallas_call(
        _merge_kernel,
        grid=(nt // tm,),
        in_specs=[
            pl.BlockSpec((tm, D_MODEL), tok),
            pl.BlockSpec((NSA_HEADS, tm, HEAD_DIM), hm_tok),
            pl.BlockSpec((NSA_HEADS, tm, HEAD_DIM), hm_tok),
            pl.BlockSpec((NSA_HEADS, tm, HEAD_DIM), hm_tok),
            pl.BlockSpec((SB_HEADS, tm, HEAD_DIM), hm_tok),
            pl.BlockSpec((tm, LANES), tok),
            pl.BlockSpec((1, D_MODEL), const2),
            pl.BlockSpec((D_MODEL, 2 * D_MODEL), const2),
            pl.BlockSpec((NSA_HEADS, HEAD_DIM, D_MODEL), const3),
            pl.BlockSpec((SB_HEADS, HEAD_DIM, D_MODEL), const3),
            pl.BlockSpec((D_MODEL, D_MODEL), const2),
            pl.BlockSpec((1, D_MODEL), const2),
            pl.BlockSpec((D_MODEL, R_W), const2),
            pl.BlockSpec((D_MODEL, R_W), const2),
            pl.BlockSpec((1, R_W), const2),
        ],
        out_specs=[
            pl.BlockSpec((tm, D_MODEL), tok),
            pl.BlockSpec((tm, D_MODEL), tok),
            pl.BlockSpec((tm, LANES), tok),
        ],
        out_shape=[
            jax.ShapeDtypeStruct((nt, D_MODEL), F32),
            jax.ShapeDtypeStruct((nt, D_MODEL), BF16),
            jax.ShapeDtypeStruct((nt, LANES), F32),
        ],
        compiler_params=_cparams("parallel"),
        name="merge",
    )(x2, o_c, o_s, o_w, o_b, g_a, g_mix, w_gm, w_bn, w_bs, w_out, g_ffn, wr_hi, wr_lo, b_r)


def _moe_kernel(hn_ref, comb_ref, h_ref, wg_ref, wu_ref, wd_ref, gfin_ref, y_ref, acc_ref):
    e = pl.program_id(1)

    @pl.when(e == 0)
    def _():
        acc_ref[...] = jnp.zeros(acc_ref.shape, F32)

    hn = hn_ref[...]
    comb = comb_ref[...]
    lane = lax.broadcasted_iota(jnp.int32, comb.shape, 1)
    c = jnp.sum(jnp.where(lane == e, comb, 0.0), axis=-1, keepdims=True)
    a = _dot(hn, wg_ref[...])
    b = _dot(hn, wu_ref[...])
    hh = (a * jax.nn.sigmoid(a)) * b * c
    acc_ref[...] += _dot(hh.astype(BF16), wd_ref[...])

    @pl.when(e == pl.num_programs(1) - 1)
    def _():
        y_ref[...] = _rms(h_ref[...] + acc_ref[...], gfin_ref[...])


def _moe(hn, comb, h1, w_gate, w_up, w_down, g_fin, tm):
    nt = hn.shape[0]
    tok = lambda i, e: (i, 0)
    return pl.pallas_call(
        _moe_kernel,
        grid=(nt // tm, N_EXPERTS),
        in_specs=[
            pl.BlockSpec((tm, D_MODEL), tok),
            pl.BlockSpec((tm, LANES), tok),
            pl.BlockSpec((tm, D_MODEL), tok),
            pl.BlockSpec((None, D_MODEL, D_EXPERT), lambda i, e: (e, 0, 0)),
            pl.BlockSpec((None, D_MODEL, D_EXPERT), lambda i, e: (e, 0, 0)),
            pl.BlockSpec((None, D_EXPERT, D_MODEL), lambda i, e: (e, 0, 0)),
            pl.BlockSpec((1, D_MODEL), lambda i, e: (0, 0)),
        ],
        out_specs=pl.BlockSpec((tm, D_MODEL), tok),
        out_shape=jax.ShapeDtypeStruct((nt, D_MODEL), F32),
        scratch_shapes=[pltpu.VMEM((tm, D_MODEL), F32)],
        compiler_params=_cparams("parallel", "arbitrary"),
        name="moe",
    )(hn, comb, h1, w_gate, w_up, w_down, g_fin)


def _pages_feature_major(cache, width):
    pool, rows = cache.shape[1], cache.shape[2]
    return cache[0].transpose(0, 2, 3, 4, 1).reshape(pool, 2, width, rows)


def _rows_output(feat, heads):
    b, _, t = feat.shape
    return feat.reshape(b, 2, heads, HEAD_DIM, t).transpose(0, 4, 1, 2, 3)[None]


def kernel(x_prompt, x_sample, cache_nsa_cmp, cache_nsa_sel, cache_nsa_win, cache_sb, page_table, norm_mix, w_in,
           nsa_cmp_w, nsa_cmp_pe, w_br_nsa, w_br_sb, w_out, norm_ffn, w_router_group, b_router_group,
           w_router_expert, b_router_expert, w_expert_gate, w_expert_up, w_expert_down, norm_final):
    batch, seq, d = x_prompt.shape
    db, ds, _ = x_sample.shape
    depth = norm_mix.shape[0]
    n_pages = page_table.shape[1]
    past_len = n_pages * PAGE_SIZE
    w_buf = cache_nsa_win.shape[2]
    assert depth == 1 and d == D_MODEL
    assert NSA_HEADS * ds == N_COL and SB_HEADS * ds == N_COL and ds < BLK_CMP
    assert past_len % BLK_SEL == 0 and w_buf == WINDOW and seq >= WINDOW
    n_p, n_s = batch * seq, db * ds
    n_past_sel = past_len // BLK_SEL

    w = w_in[0]
    w_q = jnp.concatenate([w[:, :C_KVA], jnp.pad(w[:, C_GA:C_QKVB], ((0, 0), (0, LANES - 3 * NSA_HEADS))),
                           w[:, C_QKVB:C_KB]], axis=1).astype(BF16)
    w_kv = jnp.concatenate([w[:, C_KVA:C_GA], w[:, C_KB:C_GM]], axis=1).T.astype(BF16)
    w_gm = w[:, C_GM:].astype(BF16)
    w_bn = w_br_nsa[0].reshape(NSA_HEADS, HEAD_DIM, D_MODEL).astype(BF16)
    w_bs = w_br_sb[0].reshape(SB_HEADS, HEAD_DIM, D_MODEL).astype(BF16)
    w_o = w_out[0].astype(BF16)
    w_r = jnp.zeros((D_MODEL, R_W), F32).at[:, :N_GROUPS].set(w_router_group[0])
    w_r = w_r.at[:, LANES:LANES + N_EXPERTS].set(w_router_expert[0])
    wr_hi = w_r.astype(BF16)
    wr_lo = (w_r - wr_hi.astype(F32)).astype(BF16)
    b_r = jnp.zeros((1, R_W), F32).at[0, :N_GROUPS].set(b_router_group[0])
    b_r = b_r.at[0, LANES:LANES + N_EXPERTS].set(b_router_expert[0])
    g_mix = norm_mix[0].reshape(1, D_MODEL)
    g_ffn = norm_ffn[0].reshape(1, D_MODEL)
    g_fin = norm_final.reshape(1, D_MODEL)
    pe_t = jnp.repeat(nsa_cmp_pe[0], NSA_KV_HEADS, axis=1).reshape(BLK_CMP, KV_ROW).T
    cw = nsa_cmp_w[0]
    w_t = jnp.repeat(cw, NSA_KV_W, axis=0)
    tpos = jnp.arange(seq)
    place = (tpos[:, None] // BLK_CMP == jnp.arange(LANES)[None, :])
    wc_f = [jnp.where(place, cw[c][tpos % BLK_CMP][:, None], 0.0) for c in range(2)]
    wc = jnp.stack([p for c in range(2) for p in _split_bf16(wc_f[c])], axis=0)
    rpos = jnp.arange(PAGE_SIZE)
    per_page = PAGE_SIZE // BLK_CMP
    lane = jnp.arange(LANES)
    wsel = (jnp.where(rpos[:, None] // BLK_CMP == lane[None, :], cw[0][rpos % BLK_CMP][:, None], 0.0)
            + jnp.where(rpos[:, None] // BLK_CMP + per_page == lane[None, :], cw[1][rpos % BLK_CMP][:, None], 0.0))
    wsel = wsel.astype(BF16)

    tabq_p, tabk_p = _rope_tables(jnp.arange(seq, dtype=jnp.int32))
    tabq_s, tabk_s = _rope_tables(past_len + jnp.arange(n_s, dtype=jnp.int32) % ds)
    ind = (((lane[:, None] & 1) == 0) & ((lane[:, None] >> 1) == (tpos[None, :] // BLK_SEL))).astype(BF16)
    kk = jnp.arange(PAGE_SIZE)
    later = (kk[:, None] > kk[None, :])
    u2 = jnp.concatenate([later, jnp.ones((PAGE_SIZE, PAGE_SIZE), bool)], axis=1).astype(BF16)

    tm_p = _pick(seq, (512, 256, 128))
    tm_s = _pick(n_s, (512, 256, 128))
    qhm, g_a, kva, sbf, kvab, sbb = _project(x_prompt, tabq_p, tabk_p, g_mix, w_q, w_kv, tm_p)
    qhm_s, g_a_s, kva_s, sbf_s, _, _ = _project(x_sample.reshape(1, n_s, D_MODEL), tabq_s, tabk_s, g_mix,
                                                w_q, w_kv, tm_s)

    tq = _pick(seq, (256, 128))
    o_c, selb = _cmp_prompt(qhm, kva, pe_t, w_t, wc, batch, seq, tq)
    o_s = _nsa_prompt(qhm, kvab, 1, selb, ind, batch, seq, tq, tq)
    o_w = _nsa_prompt(qhm, kvab, 2, None, None, batch, seq, tq, tq)
    o_b = _sb_prompt(qhm, sbb, u2, batch, seq, tq, PAGE_SIZE)

    cmp_t = _pages_feature_major(cache_nsa_cmp, NSA_KV_W).reshape(-1, KV_ROW, PAGE_SIZE)
    sel_t = _pages_feature_major(cache_nsa_sel, NSA_KV_W)
    sb_t = _pages_feature_major(cache_sb, SB_W)
    win_t = cache_nsa_win[0].transpose(0, 2, 3, 4, 1).reshape(db, 2, NSA_KV_W, w_buf)
    pad_q = SUBLANES - ds
    qa_s = qhm_s[:NSA_HEADS].reshape(NSA_KV_HEADS, NSA_REP, db, ds, HEAD_DIM)
    q_rows = jnp.pad(qa_s.transpose(2, 0, 1, 3, 4), ((0, 0), (0, 0), (0, 0), (0, pad_q), (0, 0)))
    q_rows = q_rows.reshape(db, NSA_KV_HEADS, NSA_REP * SUBLANES, HEAD_DIM)
    pps_c = min(n_pages, LANES // per_page)
    assert n_pages % pps_c == 0
    oc_s, sel_s, tail_sel = _cmp_sample(page_table, cmp_t, q_rows, pe_t, w_t, wsel, past_len, pps_c)
    eye_g = jnp.eye(NSA_KV_HEADS, dtype=BF16)
    qbd = qa_s.transpose(2, 0, 1, 3, 4)[:, :, :, :, None, :] * eye_g[None, :, None, None, :, None]
    qbd = qbd.reshape(db, N_COL, NSA_KV_W)
    qb_s = qhm_s[NSA_HEADS:].reshape(SB_HEADS, db, ds, HEAD_DIM)
    eye_h = jnp.eye(SB_HEADS, dtype=BF16)
    qbd_sb = qb_s.transpose(1, 0, 2, 3)[:, :, :, None, :] * eye_h[None, :, None, :, None]
    qbd_sb = qbd_sb.reshape(db, N_COL, SB_W)

    def new_rows(feat, width):
        t = feat.reshape(2, width, db, ds).transpose(2, 0, 1, 3)
        return jnp.pad(t, ((0, 0), (0, 0), (0, 0), (0, LANES - ds)))

    col = jnp.arange(N_COL)
    col_q = col % ds
    pps = _pick(n_pages, (16, 8))
    n_steps = n_pages // pps
    sel_p = sel_s[:, :, :ds, 0:2 * n_past_sel:2]
    sel_c = jnp.broadcast_to(sel_p[:, :, None], (db, NSA_KV_HEADS, NSA_REP, ds, n_past_sel))
    sel_c = sel_c.reshape(db, N_COL, n_steps, 2 * pps).transpose(0, 2, 1, 3)
    ind_s = (jnp.arange(2 * pps)[:, None] == jnp.arange(pps * PAGE_SIZE)[None, :] // BLK_SEL).astype(BF16)
    key_l = jnp.arange(LANES)[None, :]
    new_ok = ((key_l <= col_q[:, None]) & (key_l < ds)).astype(F32)
    tail_c = jnp.broadcast_to(tail_sel[:, :, None, :ds, 0], (db, NSA_KV_HEADS, NSA_REP, ds)).reshape(db, N_COL, 1)
    tmask = new_ok[None] * tail_c
    key_w = jnp.arange(w_buf)[None, :]
    d_pos = (past_len + col_q[:, None]) - (past_len - w_buf + key_w)
    wmask = ((d_pos >= 0) & (d_pos < WINDOW)).astype(F32)
    tail_win = new_rows(kva_s[2, 0], NSA_KV_W)
    wnew = jnp.pad(kva_s[2, 0].reshape(KV_ROW, db, ds).transpose(1, 0, 2), ((0, 0), (0, 0), (LANES - ds, 0)))
    os_s, ow_s, new_win = _nsa_sample(page_table, sel_t, qbd, sel_c, ind_s, new_rows(kva_s[1, 0], NSA_KV_W), tmask,
                                      win_t, tail_win, wmask, new_ok, wnew, past_len, ds, pps)
    sb_new_ok = ((key_l < col_q[:, None]) & (key_l < ds)).astype(F32)
    ob_s = _sb_sample(page_table, sb_t, qbd_sb, new_rows(sbf_s[0], SB_W), sb_new_ok, u2, _pick(n_pages, (8,)))

    def nsa_cols_to_hm(o):
        o = o.reshape(db, NSA_KV_HEADS, HEAD_DIM, NSA_KV_HEADS, NSA_REP, ds)
        o = jnp.stack([o[:, g, :, g] for g in range(NSA_KV_HEADS)], axis=0)
        return o.transpose(0, 3, 1, 4, 2).reshape(NSA_HEADS, n_s, HEAD_DIM).astype(BF16)

    oc_hm = oc_s.reshape(db, NSA_KV_HEADS, NSA_REP, SUBLANES, HEAD_DIM)[:, :, :, :ds]
    oc_hm = oc_hm.transpose(1, 2, 0, 3, 4).reshape(NSA_HEADS, n_s, HEAD_DIM).astype(BF16)
    ob_hm = ob_s.reshape(db, SB_HEADS, HEAD_DIM, SB_HEADS, ds)
    ob_hm = jnp.stack([ob_hm[:, h, :, h] for h in range(SB_HEADS)], axis=0)
    ob_hm = ob_hm.transpose(0, 1, 3, 2).reshape(SB_HEADS, n_s, HEAD_DIM).astype(BF16)

    cat = lambda a, b: jnp.concatenate([a, b], axis=1)
    x2 = jnp.concatenate([x_prompt.reshape(n_p, D_MODEL), x_sample.reshape(n_s, D_MODEL)], axis=0)
    tm = _pick(n_s, (512, 256, 128))
    assert n_p % tm == 0
    h1, hn, comb = _merge(x2, cat(o_c, oc_hm), cat(o_s, nsa_cols_to_hm(os_s)), cat(o_w, nsa_cols_to_hm(ow_s)),
                          cat(o_b, ob_hm), jnp.concatenate([g_a, g_a_s], axis=0), g_mix, w_gm, w_bn, w_bs, w_o,
                          g_ffn, wr_hi, wr_lo, b_r, tm)
    y = _moe(hn, comb, h1, w_expert_gate[0].astype(BF16), w_expert_up[0].astype(BF16),
             w_expert_down[0].astype(BF16), g_fin, tm)

    w_keep = min(WINDOW, seq)
    sample_rows = lambda feat, heads: _rows_output(feat.reshape(2 * heads * HEAD_DIM, db, ds).transpose(1, 0, 2), heads)
    return (y[:n_p].reshape(batch, seq, D_MODEL), y[n_p:].reshape(db, ds, D_MODEL),
            _rows_output(kva[0], NSA_KV_HEADS), _rows_output(kva[1], NSA_KV_HEADS),
            _rows_output(kva[2][:, :, seq - w_keep:], NSA_KV_HEADS), _rows_output(sbf, SB_HEADS),
            sample_rows(kva_s[0, 0], NSA_KV_HEADS), sample_rows(kva_s[1, 0], NSA_KV_HEADS),
            _rows_output(new_win, NSA_KV_HEADS), sample_rows(sbf_s[0], SB_HEADS))
```
